```python
import math
import jax
import jax.numpy as jnp
from jax import lax
import numpy as np

D_MODEL = 1024
BATCH = 16
SEQ = 256
DEPTH = 2
DEC_BATCH = 2
DEC_SEQ = 4096
PAST_LEN = 256

GRID_W = 64
MLA_HEADS = 8
MLA_Q_LORA = 384
MLA_KV_LORA = 256
MLA_NOPE = 128
MLA_ROPE = 64
MLA_V = 128
ROPE_BASE = 10000.0
Q_BLOCK = 128
GDN_HEADS = 8
GDN_DK = 128
GDN_DV = 128
GDN_CONV = 5
GDN_CHUNK = 64
N_EXPERTS = 32
TOP_K = 4
D_EXPERT = 1024
SWIGLU_LIMIT = 7.0
SWIGLU_ALPHA = 1.702
MOE_BLOCK = 128
N_MLA_LAYERS = (DEPTH + 1) // 2
N_GDN_LAYERS = DEPTH // 2
EPS = 1e-6

kernel_name = 'hybrid_mla_gdn_moe_diffusion_step'


def rmsnorm(x, g):
    xf = x.astype(jnp.float32)
    y = xf * lax.rsqrt(jnp.mean(xf * xf, axis=-1, keepdims=True) + EPS)
    return (y * g.astype(jnp.float32)).astype(x.dtype)


def l2norm(x):
    xf = x.astype(jnp.float32)
    return xf * lax.rsqrt(jnp.sum(xf * xf, axis=-1, keepdims=True) + EPS)


def modulation(cond, w, b):
    m = jax.nn.silu(cond.reshape(-1, D_MODEL)) @ w + b
    return jnp.split(m[:, None, :], 6, axis=-1)


def modulate(h, shift, scale):
    return (h * (1.0 + scale) + shift).astype(h.dtype)


def axial_rope_tables(n_tokens, dtype):
    rows = n_tokens // GRID_W
    row = jnp.repeat(jnp.arange(rows, dtype=jnp.float32), GRID_W)
    col = jnp.tile(jnp.arange(GRID_W, dtype=jnp.float32), rows)
    quarter = MLA_ROPE // 4
    inv = ROPE_BASE ** (-jnp.arange(quarter, dtype=jnp.float32) / quarter)
    ar = row[:, None] * inv
    ac = col[:, None] * inv
    ang = jnp.concatenate([ar, ar, ac, ac], axis=-1)
    return jnp.cos(ang).astype(dtype), jnp.sin(ang).astype(dtype)


def apply_axial_rope(x, cos, sin):
    xs = x.reshape(x.shape[:-1] + (2, 2, MLA_ROPE // 4))
    rot = jnp.stack([-xs[..., 1, :], xs[..., 0, :]], axis=-2).reshape(x.shape)
    return x * cos + rot * sin


def blocked_attention(q, k, v):
    b, tq, h, dh = q.shape
    dv = v.shape[-1]
    nb = tq // Q_BLOCK
    scale = dh ** -0.5
    qb = q.reshape(b, nb, Q_BLOCK, h, dh).transpose(1, 0, 2, 3, 4)

    def one_block(q_blk):
        s = jnp.einsum('bqhd,bkhd->bhqk', q_blk, k).astype(jnp.float32) * scale
        p = jax.nn.softmax(s, axis=-1).astype(v.dtype)
        return jnp.einsum('bhqk,bkhd->bqhd', p, v)

    out = lax.map(one_block, qb)
    return out.transpose(1, 0, 2, 3, 4).reshape(b, tq, h, dv)


def mla_project(h, w_in, g_q, g_kv, w_qb):
    a = h @ w_in
    q_a, ckv, k_rope = jnp.split(a, [MLA_Q_LORA, MLA_Q_LORA + MLA_KV_LORA], axis=-1)
    q = (rmsnorm(q_a, g_q) @ w_qb).reshape(h.shape[:2] + (MLA_HEADS, MLA_NOPE + MLA_ROPE))
    return q, rmsnorm(ckv, g_kv), k_rope


def mla_keys_values(ckv, k_rope, w_kvb):
    kv = (ckv @ w_kvb).reshape(ckv.shape[:2] + (MLA_HEADS, MLA_NOPE + MLA_V))
    k_nope, v = jnp.split(kv, [MLA_NOPE], axis=-1)
    k_r = jnp.broadcast_to(k_rope[:, :, None, :], k_nope.shape[:3] + (MLA_ROPE,))
    return jnp.concatenate([k_nope, k_r], axis=-1), v


def mla_context(h, w_in, g_q, g_kv, w_qb, w_kvb, w_o):
    b, t, _ = h.shape
    q, ckv, k_rope = mla_project(h, w_in, g_q, g_kv, w_qb)
    k, v = mla_keys_values(ckv, k_rope, w_kvb)
    o = blocked_attention(q, k, v)
    return o.reshape(b, t, MLA_HEADS * MLA_V) @ w_o, ckv, k_rope


def mla_latent(h, ckv_ctx, krope_ctx, w_in, g_q, g_kv, w_qb, w_kvb, w_o):
    b, t, _ = h.shape
    q, ckv, k_rope = mla_project(h, w_in, g_q, g_kv, w_qb)
    cos, sin = axial_rope_tables(t, h.dtype)
    q = jnp.concatenate([q[..., :MLA_NOPE],
                         apply_axial_rope(q[..., MLA_NOPE:], cos[:, None, :], sin[:, None, :])], axis=-1)
    k_rope = apply_axial_rope(k_rope, cos, sin)
    k_c, v_c = mla_keys_values(ckv_ctx, krope_ctx, w_kvb)
    k_l, v_l = mla_keys_values(ckv, k_rope, w_kvb)
    o = blocked_attention(q, jnp.concatenate([k_c, k_l], axis=1), jnp.concatenate([v_c, v_l], axis=1))
    return o.reshape(b, t, MLA_HEADS * MLA_V) @ w_o


def centred_depthwise_conv(x, w):
    pad = (GDN_CONV - 1) // 2
    return lax.conv_general_dilated(x, w[:, None, :].astype(x.dtype), window_strides=(1,),
                                    padding=[(pad, pad)], dimension_numbers=('NWC', 'WIO', 'NWC'),
                                    feature_group_count=x.shape[-1])


def chunk_gated_delta(q, k, v, g, beta, s0):
    b, t, h, dk = q.shape
    dv = v.shape[-1]
    c = GDN_CHUNK
    n = t // c

    def chunks(x):
        return x.reshape(b, n, c, h, -1).transpose(1, 0, 3, 2, 4)

    q = chunks(q) * (dk ** -0.5)
    k = chunks(k)
    v = chunks(v)
    g = chunks(g[..., None])[..., 0]
    beta = chunks(beta[..., None])[..., 0]
    gc = jnp.cumsum(g, axis=-1)
    incl = jnp.tril(jnp.ones((c, c), dtype=bool))
    strict = jnp.tril(jnp.ones((c, c), dtype=bool), -1)
    diff = gc[..., :, None] - gc[..., None, :]
    decay = jnp.where(incl, jnp.exp(jnp.where(incl, diff, 0.0)), 0.0)
    k_beta = k * beta[..., None]
    a = jnp.where(strict, jnp.einsum('nbhid,nbhjd->nbhij', k_beta, k) * decay, 0.0)
    lhs = a + jnp.eye(c, dtype=a.dtype)
    rhs = jnp.concatenate([v * beta[..., None], k_beta * jnp.exp(gc)[..., None]], axis=-1)
    uw = lax.linalg.triangular_solve(lhs, rhs, left_side=True, lower=True, unit_diagonal=True)
    u, w = jnp.split(uw, [dv], axis=-1)
    intra = jnp.where(incl, jnp.einsum('nbhid,nbhjd->nbhij', q, k) * decay, 0.0)

    def step(s, xs):
        q_c, k_c, u_c, w_c, g_c, intra_c = xs
        v_new = u_c - jnp.einsum('bhck,bhkv->bhcv', w_c, s)
        o = (jnp.einsum('bhck,bhkv->bhcv', q_c * jnp.exp(g_c)[..., None], s)
             + jnp.einsum('bhij,bhjv->bhiv', intra_c, v_new))
        g_last = g_c[..., -1:]
        s = (s * jnp.exp(g_last)[..., None]
             + jnp.einsum('bhck,bhcv->bhkv', k_c * jnp.exp(g_last - g_c)[..., None], v_new))
        return s, o

    s, o = lax.scan(step, s0, (q, k, u, w, gc, intra))
    return o.transpose(1, 0, 3, 2, 4).reshape(b, t, h, dv), s


def gdn_mixer(h, s_fwd, s_bwd, w_in, conv_w, a_log, dt_bias, g_out, w_o):
    b, t, _ = h.shape
    hdk = GDN_HEADS * GDN_DK
    hdv = GDN_HEADS * GDN_DV
    proj = h @ w_in
    qkv, z, ab = jnp.split(proj, [2 * hdk + hdv, 2 * hdk + 2 * hdv], axis=-1)
    qkv = jax.nn.silu(centred_depthwise_conv(qkv, conv_w))
    q, k, v = jnp.split(qkv, [hdk, 2 * hdk], axis=-1)
    q = l2norm(q.reshape(b, t, GDN_HEADS, GDN_DK))
    k = l2norm(k.reshape(b, t, GDN_HEADS, GDN_DK))
    v = v.reshape(b, t, GDN_HEADS, GDN_DV).astype(jnp.float32)
    ab = ab.astype(jnp.float32).reshape(b, t, 2, 2, GDN_HEADS)
    g = -jnp.exp(a_log.astype(jnp.float32)) * jax.nn.softplus(ab[:, :, 0] + dt_bias.astype(jnp.float32))
    beta = jax.nn.sigmoid(ab[:, :, 1])
    o_f, s_fwd = chunk_gated_delta(q, k, v, g[:, :, 0], beta[:, :, 0], s_fwd.astype(jnp.float32))
    o_b, s_bwd = chunk_gated_delta(jnp.flip(q, 1), jnp.flip(k, 1), jnp.flip(v, 1),
                                   jnp.flip(g[:, :, 1], 1), jnp.flip(beta[:, :, 1], 1),
                                   s_bwd.astype(jnp.float32))
    o = o_f + jnp.flip(o_b, 1)
    y = rmsnorm(o, g_out) * jax.nn.silu(z.reshape(b, t, GDN_HEADS, GDN_DV).astype(jnp.float32))
    return y.reshape(b, t, hdv).astype(h.dtype) @ w_o, s_fwd, s_bwd


def moe(h, w_router, b_router, w_gu, b_gu, w_down, b_down):
    b, t, d = h.shape
    x = h.reshape(-1, d)
    n_tok = x.shape[0]
    logits = (x @ w_router + b_router).astype(jnp.float32)
    top_v, top_i = lax.top_k(logits, TOP_K)
    gates = jax.nn.softmax(top_v, axis=-1)
    n_assign = n_tok * TOP_K
    flat_e = top_i.reshape(-1)
    order = jnp.argsort(flat_e)
    sorted_e = flat_e[order]
    counts = jnp.bincount(flat_e, length=N_EXPERTS)
    padded = (counts + MOE_BLOCK - 1) // MOE_BLOCK * MOE_BLOCK
    ends_p = jnp.cumsum(padded)
    starts_p = ends_p - padded
    starts = jnp.cumsum(counts) - counts
    dest = starts_p[sorted_e] + jnp.arange(n_assign) - starts[sorted_e]
    n_blocks = -(-n_assign // MOE_BLOCK) + N_EXPERTS
    n_rows = n_blocks * MOE_BLOCK
    row_tok = jnp.zeros((n_rows,), jnp.int32).at[dest].set((order // TOP_K).astype(jnp.int32))
    row_w = jnp.zeros((n_rows,), jnp.float32).at[dest].set(gates.reshape(-1)[order])
    block_e = jnp.minimum(jnp.searchsorted(ends_p, jnp.arange(n_blocks) * MOE_BLOCK, side='right'),
                          N_EXPERTS - 1)
    xb = x[row_tok].reshape(n_blocks, MOE_BLOCK, d)

    def expert_block(args):
        x_blk, e = args
        gu = x_blk @ w_gu[e] + b_gu[e]
        gate = jnp.minimum(gu[..., 0::2], SWIGLU_LIMIT)
        up = jnp.clip(gu[..., 1::2], -SWIGLU_LIMIT, SWIGLU_LIMIT)
        return ((up + 1.0) * (gate * jax.nn.sigmoid(SWIGLU_ALPHA * gate))) @ w_down[e] + b_down[e]

    yb = lax.map(expert_block, (xb, block_e)).reshape(n_rows, d)
    out = jnp.zeros((n_tok, d), jnp.float32).at[row_tok].add(yb.astype(jnp.float32) * row_w[:, None])
    return out.astype(h.dtype).reshape(b, t, d)


def setup_inputs(seed: int = 0) -> dict:
    key = jax.random.key(seed)
    ks = iter(jax.random.split(key, 48))
    f32 = jnp.float32
    D = D_MODEL
    hdk = GDN_HEADS * GDN_DK
    hdv = GDN_HEADS * GDN_DV
    gdn_in = 2 * hdk + 2 * hdv + 4 * GDN_HEADS

    def nrm(shape, scale):
        return jax.random.normal(next(ks), shape, f32) * scale

    def gain(shape):
        return 1.0 + 0.05 * jax.random.normal(next(ks), shape, f32)

    dt = jnp.exp(jax.random.uniform(next(ks), (N_GDN_LAYERS, 2, GDN_HEADS), f32,
                                    minval=math.log(1e-3), maxval=math.log(1e-1)))
    return {
        'x_prompt': nrm((BATCH, SEQ, D), 1.0),
        'x_sample': nrm((DEC_BATCH, DEC_SEQ, D), 1.0),
        'c': nrm((DEC_BATCH, D), 1.0),
        'cache_ckv': nrm((DEC_BATCH, N_MLA_LAYERS, PAST_LEN, MLA_KV_LORA), 1.0),
        'cache_krope': nrm((DEC_BATCH, N_MLA_LAYERS, PAST_LEN, MLA_ROPE), 1.0),
        'state_fwd': nrm((DEC_BATCH, N_GDN_LAYERS, GDN_HEADS, GDN_DK, GDN_DV), 0.1),
        'state_bwd': nrm((DEC_BATCH, N_GDN_LAYERS, GDN_HEADS, GDN_DK, GDN_DV), 0.1),
        'c_ctx': nrm((D,), 1.0),
        'w_mod': nrm((DEPTH, D, 6 * D), 0.5 * D ** -0.5),
        'b_mod': nrm((DEPTH, 6 * D), 0.02),
        'g_norm1': gain((DEPTH, D)),
        'g_norm2': gain((DEPTH, D)),
        'mla_w_in': nrm((N_MLA_LAYERS, D, MLA_Q_LORA + MLA_KV_LORA + MLA_ROPE), D ** -0.5),
        'mla_g_q': gain((N_MLA_LAYERS, MLA_Q_LORA)),
        'mla_g_kv': gain((N_MLA_LAYERS, MLA_KV_LORA)),
        'mla_w_qb': nrm((N_MLA_LAYERS, MLA_Q_LORA, MLA_HEADS * (MLA_NOPE + MLA_ROPE)), MLA_Q_LORA ** -0.5),
        'mla_w_kvb': nrm((N_MLA_LAYERS, MLA_KV_LORA, MLA_HEADS * (MLA_NOPE + MLA_V)), MLA_KV_LORA ** -0.5),
        'mla_w_o': nrm((N_MLA_LAYERS, MLA_HEADS * MLA_V, D), (MLA_HEADS * MLA_V) ** -0.5),
        'gdn_w_in': nrm((N_GDN_LAYERS, D, gdn_in), D ** -0.5),
        'gdn_conv': nrm((N_GDN_LAYERS, GDN_CONV, 2 * hdk + hdv), GDN_CONV ** -0.5),
        'gdn_a_log': jnp.log(jax.random.uniform(next(ks), (N_GDN_LAYERS, 2, GDN_HEADS), f32,
                                                minval=1.0, maxval=16.0)),
        'gdn_dt_bias': dt + jnp.log(-jnp.expm1(-dt)),
        'gdn_g_out': gain((N_GDN_LAYERS, GDN_DV)),
        'gdn_w_o': nrm((N_GDN_LAYERS, hdv, D), hdv ** -0.5),
        'moe_w_router': nrm((DEPTH, D, N_EXPERTS), D ** -0.5),
        'moe_b_router': nrm((DEPTH, N_EXPERTS), 0.01),
        'moe_w_gu': nrm((DEPTH, N_EXPERTS, D, 2 * D_EXPERT), D ** -0.5),
        'moe_b_gu': nrm((DEPTH, N_EXPERTS, 2 * D_EXPERT), 0.02),
        'moe_w_down': nrm((DEPTH, N_EXPERTS, D_EXPERT, D), D_EXPERT ** -0.5),
        'moe_b_down': nrm((DEPTH, N_EXPERTS, D), 0.02),
        'g_final': gain((D,)),
    }


def reference(x_prompt, x_sample, c, cache_ckv, cache_krope, state_fwd, state_bwd, c_ctx,
              w_mod, b_mod, g_norm1, g_norm2,
              mla_w_in, mla_g_q, mla_g_kv, mla_w_qb, mla_w_kvb, mla_w_o,
              gdn_w_in, gdn_conv, gdn_a_log, gdn_dt_bias, gdn_g_out, gdn_w_o,
              moe_w_router, moe_b_router, moe_w_gu, moe_b_gu, moe_w_down, moe_b_down,
              g_final):
    xp = x_prompt
    xs = x_sample
    new_ckv, new_krope, new_sf, new_sb = [], [], [], []
    for layer in range(DEPTH):
        sh1p, sc1p, gt1p, sh2p, sc2p, gt2p = modulation(c_ctx, w_mod[layer], b_mod[layer])
        sh1s, sc1s, gt1s, sh2s, sc2s, gt2s = modulation(c, w_mod[layer], b_mod[layer])
        hp = modulate(rmsnorm(xp, g_norm1[layer]), sh1p, sc1p)
        hs = modulate(rmsnorm(xs, g_norm1[layer]), sh1s, sc1s)
        i = layer // 2
        if layer % 2 == 0:
            mla = (mla_w_in[i], mla_g_q[i], mla_g_kv[i], mla_w_qb[i], mla_w_kvb[i], mla_w_o[i])
            op, ckv, krope = mla_context(hp, *mla)
            os_ = mla_latent(hs, cache_ckv[:, i], cache_krope[:, i], *mla)
            new_ckv.append(ckv)
            new_krope.append(krope)
        else:
            gdn = (gdn_w_in[i], gdn_conv[i], gdn_a_log[i], gdn_dt_bias[i], gdn_g_out[i], gdn_w_o[i])
            zeros = jnp.zeros((xp.shape[0], GDN_HEADS, GDN_DK, GDN_DV), jnp.float32)
            op, sf, sb = gdn_mixer(hp, zeros, zeros, *gdn)
            os_, _, _ = gdn_mixer(hs, state_fwd[:, i], state_bwd[:, i], *gdn)
            new_sf.append(sf)
            new_sb.append(sb)
        xp = xp + gt1p * op
        xs = xs + gt1s * os_
        moe_w = (moe_w_router[layer], moe_b_router[layer], moe_w_gu[layer], moe_b_gu[layer],
                 moe_w_down[layer], moe_b_down[layer])
        xp = xp + gt2p * moe(modulate(rmsnorm(xp, g_norm2[layer]), sh2p, sc2p), *moe_w)
        xs = xs + gt2s * moe(modulate(rmsnorm(xs, g_norm2[layer]), sh2s, sc2s), *moe_w)
    y_prompt = rmsnorm(xp, g_final)
    y_sample = rmsnorm(xs, g_final)
    new_cache_ckv = jnp.stack(new_ckv, axis=1)
    new_cache_krope = jnp.stack(new_krope, axis=1)
    new_state_fwd = jnp.stack(new_sf, axis=1)
    new_state_bwd = jnp.stack(new_sb, axis=1)
    return (y_prompt, y_sample, new_cache_ckv, new_cache_krope, new_state_fwd, new_state_bwd)
```

```python
import functools
import math

import jax
import jax.numpy as jnp
from jax import lax
from jax.experimental import pallas as pl
from jax.experimental.pallas import tpu as pltpu

F32 = jnp.float32
BF16 = jnp.bfloat16
U32 = jnp.uint32
I32 = jnp.int32

D_MODEL = 1024
N_PROMPT_SEQ = 16
PROMPT_LEN = 256
N_LATENT_SEQ = 2
LATENT_LEN = 4096
PAST_LEN = 256
GRID_W = 64
N_HEADS = 8
MLA_Q_LORA = 384
MLA_KV_LORA = 256
MLA_NOPE = 128
MLA_ROPE = 64
MLA_V = 128
ROPE_BASE = 10000.0
GDN_DK = 128
GDN_CONV = 5
GDN_CHUNK = 64
N_EXPERTS = 32
TOP_K = 4
D_EXPERT = 1024
SWIGLU_LIMIT = 7.0
SWIGLU_ALPHA = 1.702
EPS = 1e-6

LANES = 128
SUBLANES = 8
VMEM_LIMIT = 56 * 1024 * 1024

N_PROMPT = N_PROMPT_SEQ * PROMPT_LEN
N_LATENT = N_LATENT_SEQ * LATENT_LEN
N_TOK = N_PROMPT + N_LATENT
GROUP_ROWS = 4096
ROW_TILE = 256
HEAD_PAD = 256
MOE_BLOCK = 256
N_ASSIGN = N_TOK * TOP_K
N_BLOCKS = N_ASSIGN // MOE_BLOCK + N_EXPERTS
N_SORTED = N_BLOCKS * MOE_BLOCK
HALF = D_MODEL // 2
GDN_IN_PAD = 4 * D_MODEL + LANES
NEG_BIG = -3.0e38


def _cparams(sem):
    return pltpu.CompilerParams(dimension_semantics=sem, vmem_limit_bytes=VMEM_LIMIT)


def _dot(a, b):
    return jnp.dot(a, b, preferred_element_type=F32)


def _dot_nt(a, b):
    return lax.dot_general(a, b, (((1,), (1,)), ((), ())), preferred_element_type=F32)


def _rms(x, g):
    return x * lax.rsqrt(jnp.mean(x * x, axis=-1, keepdims=True) + EPS) * g


def _split3(x):
    a = x.astype(BF16)
    r = x - a.astype(F32)
    b = r.astype(BF16)
    c = (r - b.astype(F32)).astype(BF16)
    return a, b, c


def _dot3(a, b):
    a_hi = a.astype(BF16)
    a_lo = (a - a_hi.astype(F32)).astype(BF16)
    b_hi = b.astype(BF16)
    b_lo = (b - b_hi.astype(F32)).astype(BF16)
    return _dot(a_hi, b_hi) + _dot(a_hi, b_lo) + _dot(a_lo, b_hi)


def _pack_rows(x):
    hi = lax.bitcast_convert_type(x[:, :HALF].astype(BF16).astype(F32), U32)
    lo = lax.bitcast_convert_type(x[:, HALF:].astype(BF16).astype(F32), U32)
    return hi | (lo >> 16)


def _unpack_rows(u):
    hi = lax.bitcast_convert_type(u & jnp.uint32(0xFFFF0000), F32)
    lo = lax.bitcast_convert_type(u << 16, F32)
    return hi, lo


def _mod_kernel(c_ref, w_ref, b_ref, o_ref):
    c = c_ref[...]
    s = c * jax.nn.sigmoid(c)
    o_ref[0] = _dot(s.astype(BF16), w_ref[0].astype(BF16)) + b_ref[0]


def _modulation(cond8, w_mod, b_mod):
    depth, d, n = w_mod.shape
    tn = 768
    return pl.pallas_call(
        _mod_kernel,
        grid=(depth, n // tn),
        in_specs=[pl.BlockSpec((SUBLANES, d), lambda l, j: (0, 0)),
                  pl.BlockSpec((1, d, tn), lambda l, j: (l, 0, j)),
                  pl.BlockSpec((1, 1, tn), lambda l, j: (l, 0, j))],
        out_specs=pl.BlockSpec((1, SUBLANES, tn), lambda l, j: (l, 0, j)),
        out_shape=jax.ShapeDtypeStruct((depth, SUBLANES, n), F32),
        compiler_params=_cparams(("arbitrary", "arbitrary")),
        name="modulation",
    )(cond8, w_mod, b_mod.reshape(depth, 1, n))


def _rope_pad(xr, cos, sin, first_pair):
    up = pltpu.roll(xr, LANES - 16, 1)
    dn = pltpu.roll(xr, 16, 1)
    rot = jnp.where(first_pair, -up, dn)
    return xr * cos + rot * sin


def _keys_values(ckv_n, kr, w_kvb, kcat_ref, v_ref):
    kvp = _dot(ckv_n.astype(BF16), w_kvb)
    krb = kr.astype(BF16)
    for h in range(N_HEADS):
        kcat_ref[:, HEAD_PAD * h:HEAD_PAD * h + MLA_NOPE] = kvp[:, MLA_NOPE * h:MLA_NOPE * (h + 1)].astype(BF16)
        kcat_ref[:, HEAD_PAD * h + MLA_NOPE:HEAD_PAD * (h + 1)] = krb
    v_ref[...] = kvp[:, N_HEADS * MLA_NOPE:].astype(BF16)


def _mla_front_kernel(x_ref, mod_ref, g1_ref, win_ref, gq_ref, gkv_ref, wqb_ref, wkvb_ref, cos_ref, sin_ref,
                      q_ref, kcat_ref, v_ref, ckv_ref, kr_ref):
    m = mod_ref[0]
    sh, sc = m[:, 0:D_MODEL], m[:, D_MODEL:2 * D_MODEL]
    h = _rms(x_ref[...], g1_ref[...]) * (1.0 + sc) + sh
    a = _dot(h.astype(BF16), win_ref[...])
    qa = a[:, :MLA_Q_LORA]
    ckv = a[:, MLA_Q_LORA:MLA_Q_LORA + MLA_KV_LORA]
    kr = a[:, MLA_Q_LORA + MLA_KV_LORA:]
    q = _dot(_rms(qa, gq_ref[...]).astype(BF16), wqb_ref[...]) * ((MLA_NOPE + MLA_ROPE) ** -0.5)
    ckv_n = _rms(ckv, gkv_ref[...])
    cos, sin = cos_ref[...], sin_ref[...]
    lane = lax.broadcasted_iota(I32, (1, LANES), 1)
    first_pair = (lane % 32) < 16
    for hh in range(N_HEADS):
        lo = HEAD_PAD * hh
        q_ref[:, lo:lo + MLA_NOPE] = q[:, lo:lo + MLA_NOPE].astype(BF16)
        q_ref[:, lo + MLA_NOPE:lo + HEAD_PAD] = _rope_pad(q[:, lo + MLA_NOPE:lo + HEAD_PAD], cos, sin,
                                                         first_pair).astype(BF16)
    ckv_ref[...] = ckv_n
    kr_ref[...] = kr
    _keys_values(ckv_n, _rope_pad(kr, cos, sin, first_pair), wkvb_ref[...], kcat_ref, v_ref)


def _mla_front(x, mod0, g1, w_in_p, g_q, g_kv, w_qb_p, w_kvb_p, cos, sin):
    n = x.shape[0]
    tm = ROW_TILE
    tiles_per_group = GROUP_ROWS // tm
    row = lambda w: pl.BlockSpec((tm, w), lambda i: (i, 0))
    full = lambda a: pl.BlockSpec(a.shape, lambda i: (0,) * a.ndim)
    return pl.pallas_call(
        _mla_front_kernel,
        grid=(n // tm,),
        in_specs=[row(D_MODEL),
                  pl.BlockSpec((1, 1, 6 * D_MODEL), lambda i: (i // tiles_per_group, 0, 0)),
                  full(g1), full(w_in_p), full(g_q), full(g_kv), full(w_qb_p), full(w_kvb_p),
                  row(LANES), row(LANES)],
        out_specs=[row(N_HEADS * HEAD_PAD), row(N_HEADS * HEAD_PAD), row(N_HEADS * MLA_V),
                   row(MLA_KV_LORA), row(LANES)],
        out_shape=[jax.ShapeDtypeStruct((n, N_HEADS * HEAD_PAD), BF16),
                   jax.ShapeDtypeStruct((n, N_HEADS * HEAD_PAD), BF16),
                   jax.ShapeDtypeStruct((n, N_HEADS * MLA_V), BF16),
                   jax.ShapeDtypeStruct((n, MLA_KV_LORA), F32),
                   jax.ShapeDtypeStruct((n, LANES), F32)],
        compiler_params=_cparams(("arbitrary",)),
        name="mla_front",
    )(x, mod0, g1, w_in_p, g_q, g_kv, w_qb_p, w_kvb_p, cos, sin)


def _cache_kv_kernel(ckv_ref, kr_ref, wkvb_ref, kcat_ref, v_ref):
    _keys_values(ckv_ref[...], kr_ref[...], wkvb_ref[...], kcat_ref, v_ref)


def _cache_kv(ckv, kr_pad, w_kvb_p):
    n = ckv.shape[0]
    return pl.pallas_call(
        _cache_kv_kernel,
        out_shape=[jax.ShapeDtypeStruct((n, N_HEADS * HEAD_PAD), BF16),
                   jax.ShapeDtypeStruct((n, N_HEADS * MLA_V), BF16)],
        compiler_params=pltpu.CompilerParams(vmem_limit_bytes=VMEM_LIMIT),
        name="cache_kv",
    )(ckv, kr_pad, w_kvb_p)


def _attn_kernel(*refs, n_parts, heads):
    q_ref = refs[0]
    kv_refs = refs[1:1 + 2 * n_parts]
    o_ref = refs[1 + 2 * n_parts]
    for j in range(heads):
        q = q_ref[0, :, HEAD_PAD * j:HEAD_PAD * (j + 1)]
        scores = [_dot_nt(q, kv_refs[2 * p][0, :, HEAD_PAD * j:HEAD_PAD * (j + 1)]) for p in range(n_parts)]
        m = scores[0].max(axis=-1, keepdims=True)
        for s in scores[1:]:
            m = jnp.maximum(m, s.max(axis=-1, keepdims=True))
        probs = [jnp.exp(s - m) for s in scores]
        denom = probs[0].sum(axis=-1, keepdims=True)
        for p in probs[1:]:
            denom = denom + p.sum(axis=-1, keepdims=True)
        acc = None
        for p in range(n_parts):
            t = _dot(probs[p].astype(BF16), kv_refs[2 * p + 1][0, :, MLA_V * j:MLA_V * (j + 1)])
            acc = t if acc is None else acc + t
        o_ref[0, :, MLA_V * j:MLA_V * (j + 1)] = (acc / denom).astype(BF16)


def _attention(q, kv_parts, tq, heads):
    b, t, _ = q.shape
    n_parts = len(kv_parts)
    in_specs = [pl.BlockSpec((1, tq, HEAD_PAD * heads), lambda bi, hi, qi: (bi, qi, hi))]
    args = [q]
    for k, v in kv_parts:
        tk = k.shape[1]
        in_specs.append(pl.BlockSpec((1, tk, HEAD_PAD * heads), lambda bi, hi, qi: (bi, 0, hi)))
        in_specs.append(pl.BlockSpec((1, tk, MLA_V * heads), lambda bi, hi, qi: (bi, 0, hi)))
        args += [k, v]
    return pl.pallas_call(
        functools.partial(_attn_kernel, n_parts=n_parts, heads=heads),
        grid=(b, N_HEADS // heads, t // tq),
        in_specs=in_specs,
        out_specs=pl.BlockSpec((1, tq, MLA_V * heads), lambda bi, hi, qi: (bi, qi, hi)),
        out_shape=jax.ShapeDtypeStruct((b, t, N_HEADS * MLA_V), BF16),
        compiler_params=_cparams(("arbitrary", "arbitrary", "arbitrary")),
        name=f"attention_{n_parts}",
    )(*args)


def _post_mixer_kernel(o_ref, wo_ref, x_ref, mod_ref, g2_ref, wr_hi_ref, wr_lo_ref, br_ref,
                       x1_ref, hpk_ref, eidx_ref, gate_ref, rank_ref, cnt_ref, carry_ref):
    i = pl.program_id(0)
    tm = x_ref.shape[0]

    @pl.when(i == 0)
    def _():
        carry_ref[...] = jnp.zeros_like(carry_ref)

    m = mod_ref[0]
    gt1 = m[:, 2 * D_MODEL:3 * D_MODEL]
    sh2, sc2 = m[:, 3 * D_MODEL:4 * D_MODEL], m[:, 4 * D_MODEL:5 * D_MODEL]
    x1 = x_ref[...] + gt1 * _dot(o_ref[...], wo_ref[...])
    x1_ref[...] = x1
    h2 = _rms(x1, g2_ref[...]) * (1.0 + sc2) + sh2
    hpk_ref[...] = _pack_rows(h2)

    h_hi = h2.astype(BF16)
    h_lo = (h2 - h_hi.astype(F32)).astype(BF16)
    w_hi = wr_hi_ref[...]
    logits = _dot(h_hi, w_hi) + _dot(h_lo, w_hi) + _dot(h_hi, wr_lo_ref[...]) + br_ref[...]

    lane = lax.broadcasted_iota(I32, (tm, LANES), 1)
    vals, idxs = [], []
    l = logits
    for _k in range(TOP_K):
        mx = l.max(axis=-1, keepdims=True)
        ix = jnp.where(l == mx, lane, LANES).min(axis=-1, keepdims=True)
        vals.append(mx)
        idxs.append(ix)
        l = jnp.where(lane == ix, NEG_BIG, l)
    exps = [jnp.exp(v - vals[0]) for v in vals]
    denom = exps[0] + exps[1] + exps[2] + exps[3]

    hot = jnp.zeros((tm, LANES), F32)
    for ix in idxs:
        hot = hot + (lane == ix).astype(F32)
    r_io = lax.broadcasted_iota(I32, (tm, tm), 0)
    c_io = lax.broadcasted_iota(I32, (tm, tm), 1)
    tri = (c_io < r_io).astype(BF16)
    prefix = _dot(tri, hot.astype(BF16)) + carry_ref[0:1, :]
    carry_ref[0:1, :] = carry_ref[0:1, :] + hot.sum(axis=0, keepdims=True)

    eidx = jnp.zeros((tm, LANES), I32)
    gate = jnp.zeros((tm, LANES), F32)
    rank = jnp.zeros((tm, LANES), I32)
    for k in range(TOP_K):
        rk = jnp.where(lane == idxs[k], prefix, 0.0).sum(axis=-1, keepdims=True).astype(I32)
        eidx = jnp.where(lane == k, idxs[k], eidx)
        gate = jnp.where(lane == k, exps[k] / denom, gate)
        rank = jnp.where(lane == k, rk, rank)
    eidx_ref[...] = eidx
    gate_ref[...] = gate
    rank_ref[...] = rank
    cnt_ref[...] = jnp.broadcast_to(carry_ref[0:1, :], cnt_ref.shape)


def _post_mixer(o, w_o, x, mod_l, g2, wr_hi, wr_lo, br):
    n = x.shape[0]
    tm = ROW_TILE
    tiles_per_group = GROUP_ROWS // tm
    row = lambda w: pl.BlockSpec((tm, w), lambda i: (i, 0))
    full = lambda a: pl.BlockSpec(a.shape, lambda i: (0,) * a.ndim)
    return pl.pallas_call(
        _post_mixer_kernel,
        grid=(n // tm,),
        in_specs=[row(D_MODEL), full(w_o), row(D_MODEL),
                  pl.BlockSpec((1, 1, 6 * D_MODEL), lambda i: (i // tiles_per_group, 0, 0)),
                  full(g2), full(wr_hi), full(wr_lo), full(br)],
        out_specs=[row(D_MODEL), row(HALF), row(LANES), row(LANES), row(LANES),
                   pl.BlockSpec((SUBLANES, LANES), lambda i: (0, 0))],
        out_shape=[jax.ShapeDtypeStruct((n, D_MODEL), F32),
                   jax.ShapeDtypeStruct((n, HALF), U32),
                   jax.ShapeDtypeStruct((n, LANES), I32),
                   jax.ShapeDtypeStruct((n, LANES), F32),
                   jax.ShapeDtypeStruct((n, LANES), I32),
                   jax.ShapeDtypeStruct((SUBLANES, LANES), F32)],
        scratch_shapes=[pltpu.VMEM((SUBLANES, LANES), F32)],
        compiler_params=_cparams(("arbitrary",)),
        name="post_mixer",
    )(o, w_o, x, mod_l, g2, wr_hi, wr_lo, br)


def _invert_kernel(pos_ref, rowtok_ref):
    def zero(r, c):
        rowtok_ref[r] = 0
        return c
    lax.fori_loop(0, N_SORTED, zero, 0)

    def put(a, c):
        rowtok_ref[pos_ref[a]] = a // TOP_K
        return c
    lax.fori_loop(0, N_ASSIGN, put, 0)


def _invert_positions(pos_flat):
    return pl.pallas_call(
        _invert_kernel,
        in_specs=[pl.BlockSpec(memory_space=pltpu.SMEM)],
        out_specs=pl.BlockSpec(memory_space=pltpu.SMEM),
        out_shape=jax.ShapeDtypeStruct((N_SORTED,), I32),
        name="invert_positions",
    )(pos_flat)


GATHER_ROWS = 256


def _gather_kernel(idx_ref, tab_ref, o_ref, sem):
    base = pl.program_id(0) * GATHER_ROWS

    def row_copy(r, src_row):
        return pltpu.make_async_copy(tab_ref.at[pl.ds(src_row, 1)], o_ref.at[pl.ds(r, 1)], sem)

    def issue(r, c):
        row_copy(r, idx_ref[base + r]).start()
        return c
    lax.fori_loop(0, GATHER_ROWS, issue, 0)

    def drain(r, c):
        row_copy(r, 0).wait()
        return c
    lax.fori_loop(0, GATHER_ROWS, drain, 0)


def _gather_rows(idx, table):
    n = idx.shape[0]
    return pl.pallas_call(
        _gather_kernel,
        grid_spec=pltpu.PrefetchScalarGridSpec(
            num_scalar_prefetch=1,
            grid=(n // GATHER_ROWS,),
            in_specs=[pl.BlockSpec(memory_space=pl.ANY)],
            out_specs=pl.BlockSpec((GATHER_ROWS, HALF), lambda i, idx_ref: (i, 0)),
            scratch_shapes=[pltpu.SemaphoreType.DMA(())]),
        out_shape=jax.ShapeDtypeStruct((n, HALF), U32),
        compiler_params=_cparams(("arbitrary",)),
        name="gather_rows",
    )(idx, table)


def _expert_kernel(be_ref, nu_ref, xs_ref, wgu_ref, bg_ref, bu_ref, wd_ref, bd_ref, perm_ref,
                   y_ref, wg_s, wu_s, wd_s):
    b = pl.program_id(0)
    e = be_ref[b]
    prev = be_ref[jnp.maximum(b - 1, 0)]

    @pl.when((b == 0) | (e != prev))
    def _():
        perm = perm_ref[...]
        for c in range(2 * D_EXPERT // HEAD_PAD):
            t = _dot(wgu_ref[0, :, HEAD_PAD * c:HEAD_PAD * (c + 1)].astype(BF16), perm)
            wg_s[:, LANES * c:LANES * (c + 1)] = t[:, :LANES].astype(BF16)
            wu_s[:, LANES * c:LANES * (c + 1)] = t[:, LANES:].astype(BF16)
        wd_s[...] = wd_ref[0].astype(BF16)

    @pl.when(b < nu_ref[0])
    def _():
        hi, lo = _unpack_rows(xs_ref[...])
        x = jnp.concatenate([hi.astype(BF16), lo.astype(BF16)], axis=1)
        gate = jnp.minimum(_dot(x, wg_s[...]) + bg_ref[0], SWIGLU_LIMIT)
        up = jnp.clip(_dot(x, wu_s[...]) + bu_ref[0], -SWIGLU_LIMIT, SWIGLU_LIMIT)
        hid = (up + 1.0) * (gate * jax.nn.sigmoid(SWIGLU_ALPHA * gate))
        y_ref[...] = _pack_rows(_dot(hid.astype(BF16), wd_s[...]) + bd_ref[0])

    @pl.when(b >= nu_ref[0])
    def _():
        y_ref[...] = jnp.zeros_like(y_ref)


def _experts(block_e, n_used, xs, w_gu, b_gate, b_up, w_down, b_down, perm):
    return pl.pallas_call(
        _expert_kernel,
        grid_spec=pltpu.PrefetchScalarGridSpec(
            num_scalar_prefetch=2,
            grid=(N_BLOCKS,),
            in_specs=[pl.BlockSpec((MOE_BLOCK, HALF), lambda b, be, nu: (b, 0)),
                      pl.BlockSpec((1, D_MODEL, 2 * D_EXPERT), lambda b, be, nu: (be[b], 0, 0)),
                      pl.BlockSpec((1, 1, D_EXPERT), lambda b, be, nu: (be[b], 0, 0)),
                      pl.BlockSpec((1, 1, D_EXPERT), lambda b, be, nu: (be[b], 0, 0)),
                      pl.BlockSpec((1, D_EXPERT, D_MODEL), lambda b, be, nu: (be[b], 0, 0)),
                      pl.BlockSpec((1, 1, D_MODEL), lambda b, be, nu: (be[b], 0, 0)),
                      pl.BlockSpec((HEAD_PAD, HEAD_PAD), lambda b, be, nu: (0, 0))],
            out_specs=pl.BlockSpec((MOE_BLOCK, HALF), lambda b, be, nu: (b, 0)),
            scratch_shapes=[pltpu.VMEM((D_MODEL, D_EXPERT), BF16),
                            pltpu.VMEM((D_MODEL, D_EXPERT), BF16),
                            pltpu.VMEM((D_EXPERT, D_MODEL), BF16)]),
        out_shape=jax.ShapeDtypeStruct((N_SORTED, HALF), U32),
        compiler_params=_cparams(("arbitrary",)),
        name="experts",
    )(block_e, n_used, xs, w_gu, b_gate, b_up, w_down, b_down, perm)


def _moe(hpk, eidx, rank, counts, w_gu, b_gu, w_down, b_down, perm):
    cnt = counts[0, :N_EXPERTS].astype(I32)
    padded = (cnt + MOE_BLOCK - 1) // MOE_BLOCK * MOE_BLOCK
    ends = jnp.cumsum(padded)
    starts = ends - padded
    pos = (starts[eidx[:, :TOP_K]] + rank[:, :TOP_K]).reshape(-1)
    block_e = jnp.minimum(jnp.searchsorted(ends, jnp.arange(N_BLOCKS, dtype=I32) * MOE_BLOCK, side='right'),
                          N_EXPERTS - 1).astype(I32)
    n_used = (ends[-1:] // MOE_BLOCK).astype(I32)
    row_tok = _invert_positions(pos)
    xs = _gather_rows(row_tok, hpk)
    b_gate = b_gu[:, 0::2].reshape(N_EXPERTS, 1, D_EXPERT)
    b_up = b_gu[:, 1::2].reshape(N_EXPERTS, 1, D_EXPERT)
    ys = _experts(block_e, n_used, xs, w_gu, b_gate, b_up, w_down, b_down.reshape(N_EXPERTS, 1, D_MODEL), perm)
    return _gather_rows(pos, ys).reshape(N_TOK, TOP_K * HALF)


def _moe_sum(yg_ref, gate_ref):
    g = gate_ref[...]
    acc_hi = acc_lo = None
    for k in range(TOP_K):
        hi, lo = _unpack_rows(yg_ref[:, HALF * k:HALF * (k + 1)])
        gk = g[:, k:k + 1]
        acc_hi = gk * hi if acc_hi is None else acc_hi + gk * hi
        acc_lo = gk * lo if acc_lo is None else acc_lo + gk * lo
    return jnp.concatenate([acc_hi, acc_lo], axis=1)


def _combine_gdn_kernel(x1_ref, yg_ref, gate_ref, mod0_ref, mod1_ref, g1_ref, win_ref, x2_ref, proj_ref):
    gt2 = mod0_ref[0][:, 5 * D_MODEL:6 * D_MODEL]
    x2 = x1_ref[...] + gt2 * _moe_sum(yg_ref, gate_ref)
    x2_ref[...] = x2
    m1 = mod1_ref[0]
    h = _rms(x2, g1_ref[...]) * (1.0 + m1[:, D_MODEL:2 * D_MODEL]) + m1[:, 0:D_MODEL]
    proj_ref[...] = _dot(h.astype(BF16), win_ref[...])


def _combine_gdn(x1, yg, gates, mod0, mod1, g1, w_in):
    n = x1.shape[0]
    tm = ROW_TILE
    tiles_per_group = GROUP_ROWS // tm
    row = lambda w: pl.BlockSpec((tm, w), lambda i: (i, 0))
    full = lambda a: pl.BlockSpec(a.shape, lambda i: (0,) * a.ndim)
    modspec = pl.BlockSpec((1, 1, 6 * D_MODEL), lambda i: (i // tiles_per_group, 0, 0))
    return pl.pallas_call(
        _combine_gdn_kernel,
        grid=(n // tm,),
        in_specs=[row(D_MODEL), row(TOP_K * HALF), row(LANES), modspec, modspec, full(g1), full(w_in)],
        out_specs=[row(D_MODEL), row(GDN_IN_PAD)],
        out_shape=[jax.ShapeDtypeStruct((n, D_MODEL), F32),
                   jax.ShapeDtypeStruct((n, GDN_IN_PAD), F32)],
        compiler_params=_cparams(("arbitrary",)),
        name="combine_gdn_proj",
    )(x1, yg, gates, mod0, mod1, g1, w_in)


def _combine_final_kernel(x1_ref, yg_ref, gate_ref, mod_ref, gf_ref, y_ref):
    gt2 = mod_ref[0][:, 5 * D_MODEL:6 * D_MODEL]
    x2 = x1_ref[...] + gt2 * _moe_sum(yg_ref, gate_ref)
    y_ref[...] = _rms(x2, gf_ref[...])


def _combine_final(x1, yg, gates, mod_l, g_final):
    n = x1.shape[0]
    tm = ROW_TILE
    tiles_per_group = GROUP_ROWS // tm
    row = lambda w: pl.BlockSpec((tm, w), lambda i: (i, 0))
    return pl.pallas_call(
        _combine_final_kernel,
        grid=(n // tm,),
        in_specs=[row(D_MODEL), row(TOP_K * HALF), row(LANES),
                  pl.BlockSpec((1, 1, 6 * D_MODEL), lambda i: (i // tiles_per_group, 0, 0)),
                  pl.BlockSpec(g_final.shape, lambda i: (0, 0))],
        out_specs=row(D_MODEL),
        out_shape=jax.ShapeDtypeStruct((n, D_MODEL), F32),
        compiler_params=_cparams(("arbitrary",)),
        name="combine_final",
    )(x1, yg, gates, mod_l, g_final)


GDN_TILE = 256
GDN_ROWS = 4096
CHUNKS_PER_TILE = GDN_TILE // GDN_CHUNK
CONV_PAD = 8


def _gdn_kernel(*refs, seq_len, sb_rows, has_init):
    if has_init:
        (q_in, k_in, v_in, z_in, ab_in, cwq, cwk, cwv, alog, dtb, gout, s0f_ref, s0b_ref,
         y_ref, sf_ref, sb_ref, *scr) = refs
    else:
        (q_in, k_in, v_in, z_in, ab_in, cwq, cwk, cwv, alog, dtb, gout,
         y_ref, sf_ref, sb_ref, *scr) = refs
        s0f_ref = s0b_ref = None
    (xpad, qn, kn, vv, gb, o_f, o_b, lhs_s, co_s, cs_s, d_s, st_f, st_b) = scr

    head = pl.program_id(1)
    n_tiles = GDN_ROWS // GDN_TILE
    n_sb = GDN_ROWS // sb_rows
    ncs = sb_rows // GDN_CHUNK
    tiles_per_sb = sb_rows // GDN_TILE
    sb_per_seq = seq_len // sb_rows

    row_t = lax.broadcasted_iota(I32, (GDN_TILE, 1), 0)
    xpad[0:CONV_PAD, :] = jnp.zeros((CONV_PAD, LANES), F32)
    xpad[CONV_PAD + GDN_ROWS:, :] = jnp.zeros((CONV_PAD, LANES), F32)
    half = (GDN_CONV - 1) // 2
    for src, cw, dst, unit, scale in ((q_in, cwq, qn, True, GDN_DK ** -0.5), (k_in, cwk, kn, True, 1.0),
                                      (v_in, cwv, vv, False, 1.0)):
        xpad[CONV_PAD:CONV_PAD + GDN_ROWS, :] = src[...]
        w = cw[...]
        for t in range(n_tiles):
            r0 = t * GDN_TILE
            pos = (r0 + row_t) % seq_len
            acc = jnp.zeros((GDN_TILE, LANES), F32)
            for j in range(GDN_CONV):
                s = j - half
                xs = xpad[CONV_PAD + r0 + s:CONV_PAD + r0 + s + GDN_TILE, :]
                ok = (pos + s >= 0) & (pos + s < seq_len)
                acc = acc + jnp.where(ok, xs, 0.0) * w[j:j + 1, :]
            acc = acc * jax.nn.sigmoid(acc)
            if unit:
                acc = acc * lax.rsqrt(jnp.sum(acc * acc, axis=-1, keepdims=True) + EPS) * scale
            dst[r0:r0 + GDN_TILE, :] = acc
    lane1 = lax.broadcasted_iota(I32, (1, LANES), 1)
    ab = ab_in[...]
    sp = ab + dtb[...]
    softplus = jnp.maximum(sp, 0.0) + jnp.log(1.0 + jnp.exp(-jnp.abs(sp)))
    gb[...] = jnp.where(lane1 < 2 * N_HEADS, -jnp.exp(alog[...]) * softplus, jax.nn.sigmoid(ab))

    r_io = lax.broadcasted_iota(I32, (GDN_TILE, GDN_TILE), 0)
    c_io = lax.broadcasted_iota(I32, (GDN_TILE, GDN_TILE), 1)
    same = (r_io // GDN_CHUNK) == (c_io // GDN_CHUNK)
    eye = (r_io == c_io).astype(F32)
    lane_t = lax.broadcasted_iota(I32, (GDN_TILE, LANES), 1)
    colc = lax.broadcasted_iota(I32, (LANES, GDN_TILE), 1) // GDN_CHUNK

    def prep_tile(direction, r0, slot0):
        if direction == 0:
            incl = same & (c_io <= r_io)
            strict = same & (c_io < r_io)
        else:
            incl = same & (c_io >= r_io)
            strict = same & (c_io > r_io)
        q = qn[pl.ds(r0, GDN_TILE), :]
        k = kn[pl.ds(r0, GDN_TILE), :]
        v = vv[pl.ds(r0, GDN_TILE), :]
        gbt = gb[pl.ds(r0, GDN_TILE), :]
        col_g = N_HEADS * direction + head
        g_col = jnp.where(lane_t == col_g, gbt, 0.0).sum(axis=-1, keepdims=True)
        beta = jnp.where(lane_t == col_g + 2 * N_HEADS, gbt, 0.0).sum(axis=-1, keepdims=True)
        g_b = jnp.broadcast_to(g_col, (GDN_TILE, LANES))
        g3 = _split3(g_b)
        cs_m = incl.astype(BF16)
        all_m = same.astype(BF16)
        gc = _dot(cs_m, g3[0]) + _dot(cs_m, g3[1]) + _dot(cs_m, g3[2])
        gl = _dot(all_m, g3[0]) + _dot(all_m, g3[1]) + _dot(all_m, g3[2])
        gc_row = gc.T[0:1, :]
        diff = jnp.concatenate([gc, gc], axis=1) - gc_row
        decay = jnp.where(incl, jnp.exp(jnp.where(incl, diff, 0.0)), 0.0)
        e_gc = jnp.exp(gc)
        kb = k * beta
        kbf = k.astype(BF16)
        a = jnp.where(strict, _dot_nt(kb.astype(BF16), kbf) * decay, 0.0)
        tinv = eye - jnp.where((r_io // 2) == (c_io // 2), a, 0.0)
        size = 2
        while size < GDN_CHUNK:
            off = jnp.where(((r_io // (2 * size)) == (c_io // (2 * size))) & ((r_io // size) != (c_io // size)),
                            a, 0.0)
            tinv = tinv - _dot3(tinv, _dot3(off, tinv))
            size *= 2
        rhs = jnp.concatenate([v * beta, kb * e_gc], axis=1)
        uw = _dot3(tinv, rhs)
        intra = jnp.where(incl, _dot_nt(q.astype(BF16), kbf) * decay, 0.0)
        uwb = uw.astype(BF16)
        iuw = _dot(intra.astype(BF16), uwb)
        c_o = iuw[:, :LANES]
        q_eff = q * e_gc - iuw[:, LANES:]
        kd_t = (k * jnp.exp(gl - gc)).T
        dec = jnp.exp(gl)
        for mch in range(CHUNKS_PER_TILE):
            lo = mch * GDN_CHUNK
            gcs = _dot(jnp.where(colc == mch, kd_t, 0.0).astype(BF16), uwb)
            slot = slot0 + mch
            lhs_s[direction, slot, 0:GDN_CHUNK, :] = q_eff[lo:lo + GDN_CHUNK, :].astype(BF16)
            lhs_s[direction, slot, GDN_CHUNK:, :] = gcs[:, LANES:].astype(BF16)
            co_s[direction, slot] = c_o[lo:lo + GDN_CHUNK, :]
            cs_s[direction, slot] = gcs[:, :LANES]
            d_s[direction, slot] = dec[lo:lo + SUBLANES, :]

    def superblock(j, carry):
        sbs = (j, n_sb - 1 - j)
        for direction in (0, 1):
            base = sbs[direction] * sb_rows
            for t in range(tiles_per_sb):
                prep_tile(direction, pl.multiple_of(base + t * GDN_TILE, GDN_TILE), t * CHUNKS_PER_TILE)
        for direction, st, s0 in ((0, st_f, s0f_ref), (1, st_b, s0b_ref)):
            sbi = sbs[direction]
            first = (sbi % sb_per_seq == 0) if direction == 0 else ((sbi + 1) % sb_per_seq == 0)

            @pl.when(first)
            def _():
                st[...] = s0[0, 0] if has_init else jnp.zeros((GDN_DK, LANES), F32)

        def step(i, c):
            for direction, st, o_s in ((0, st_f, o_f), (1, st_b, o_b)):
                slot = i if direction == 0 else ncs - 1 - i
                s = st[...]
                r = _dot(lhs_s[direction, slot], s.astype(BF16))
                row0 = pl.multiple_of(sbs[direction] * sb_rows + slot * GDN_CHUNK, GDN_CHUNK)
                o_s[pl.ds(row0, GDN_CHUNK), :] = r[:GDN_CHUNK, :] + co_s[direction, slot]
                st[...] = d_s[direction, slot][0:1, :] * s - r[GDN_CHUNK:, :] + cs_s[direction, slot]
            return c
        lax.fori_loop(0, ncs, step, 0)

        for direction, st, out in ((0, st_f, sf_ref), (1, st_b, sb_ref)):
            sbi = sbs[direction]
            last = ((sbi + 1) % sb_per_seq == 0) if direction == 0 else (sbi % sb_per_seq == 0)

            @pl.when(last)
            def _():
                out[sbi // sb_per_seq, 0] = st[...]
        return carry
    lax.fori_loop(0, n_sb, superblock, 0)

    for t in range(n_tiles):
        r0 = t * GDN_TILE
        o = o_f[r0:r0 + GDN_TILE, :] + o_b[r0:r0 + GDN_TILE, :]
        z = z_in[r0:r0 + GDN_TILE, :]
        y_ref[r0:r0 + GDN_TILE, :] = (_rms(o, gout[...]) * (z * jax.nn.sigmoid(z))).astype(BF16)


def _gdn(proj, conv_w8, alog, dtb, gout, row_block0, n_blocks, seq_len, sb_rows, s0f=None, s0b=None):
    has_init = s0f is not None
    seqs = GDN_ROWS // seq_len
    ncs = sb_rows // GDN_CHUNK
    col = lambda off: pl.BlockSpec((GDN_ROWS, LANES), lambda b, h: (b + row_block0, off + h))
    cwspec = lambda off: pl.BlockSpec((SUBLANES, LANES), lambda b, h: (0, off + h))
    vec = pl.BlockSpec((1, LANES), lambda b, h: (0, 0))
    in_specs = [col(0), col(N_HEADS), col(2 * N_HEADS), col(3 * N_HEADS),
                pl.BlockSpec((GDN_ROWS, LANES), lambda b, h: (b + row_block0, 4 * N_HEADS)),
                cwspec(0), cwspec(N_HEADS), cwspec(2 * N_HEADS), vec, vec, vec]
    args = [proj, proj, proj, proj, proj, conv_w8, conv_w8, conv_w8, alog, dtb, gout]
    if has_init:
        st_spec = pl.BlockSpec((1, 1, GDN_DK, LANES), lambda b, h: (b, h, 0, 0))
        in_specs += [st_spec, st_spec]
        args += [s0f, s0b]
    out_state = pl.BlockSpec((seqs, 1, GDN_DK, LANES), lambda b, h: (b, h, 0, 0))
    return pl.pallas_call(
        functools.partial(_gdn_kernel, seq_len=seq_len, sb_rows=sb_rows, has_init=has_init),
        grid=(n_blocks, N_HEADS),
        in_specs=in_specs,
        out_specs=[pl.BlockSpec((GDN_ROWS, LANES), lambda b, h: (b, h)), out_state, out_state],
        out_shape=[jax.ShapeDtypeStruct((n_blocks * GDN_ROWS, N_HEADS * LANES), BF16),
                   jax.ShapeDtypeStruct((n_blocks * seqs, N_HEADS, GDN_DK, LANES), F32),
                   jax.ShapeDtypeStruct((n_blocks * seqs, N_HEADS, GDN_DK, LANES), F32)],
        scratch_shapes=[pltpu.VMEM((GDN_ROWS + 2 * CONV_PAD, LANES), F32),
                        pltpu.VMEM((GDN_ROWS, LANES), F32), pltpu.VMEM((GDN_ROWS, LANES), F32),
                        pltpu.VMEM((GDN_ROWS, LANES), F32), pltpu.VMEM((GDN_ROWS, LANES), F32),
                        pltpu.VMEM((GDN_ROWS, LANES), F32), pltpu.VMEM((GDN_ROWS, LANES), F32),
                        pltpu.VMEM((2, ncs, GDN_CHUNK + GDN_DK, LANES), BF16),
                        pltpu.VMEM((2, ncs, GDN_CHUNK, LANES), F32),
                        pltpu.VMEM((2, ncs, GDN_DK, LANES), F32),
                        pltpu.VMEM((2, ncs, SUBLANES, LANES), F32),
                        pltpu.VMEM((GDN_DK, LANES), F32), pltpu.VMEM((GDN_DK, LANES), F32)],
        compiler_params=_cparams(("arbitrary", "arbitrary")),
        name=f"gdn_{seq_len}",
    )(*args)


def _rope_tables():
    rows = LATENT_LEN // GRID_W
    row = jnp.repeat(jnp.arange(rows, dtype=F32), GRID_W)
    colp = jnp.tile(jnp.arange(GRID_W, dtype=F32), rows)
    quarter = MLA_ROPE // 4
    inv = ROPE_BASE ** (-jnp.arange(quarter, dtype=F32) / quarter)
    ar = row[:, None] * inv
    ac = colp[:, None] * inv
    ang = jnp.concatenate([ar, ar, ac, ac], axis=-1)
    pad = ((0, 0), (0, LANES - MLA_ROPE))
    cos = jnp.pad(jnp.cos(ang), pad, constant_values=1.0)
    sin = jnp.pad(jnp.sin(ang), pad)
    cos = jnp.concatenate([jnp.ones((N_PROMPT, LANES), F32)] + [cos] * N_LATENT_SEQ, axis=0)
    sin = jnp.concatenate([jnp.zeros((N_PROMPT, LANES), F32)] + [sin] * N_LATENT_SEQ, axis=0)
    return cos, sin


def _split_perm():
    src = jnp.arange(HEAD_PAD)
    dst = jnp.where(src % 2 == 0, src // 2, LANES + src // 2)
    return (dst[:, None] == jnp.arange(HEAD_PAD)[None, :]).astype(BF16)


def _router_operands(w_router, b_router):
    w = jnp.pad(w_router, ((0, 0), (0, LANES - N_EXPERTS)))
    w_hi = w.astype(BF16)
    w_lo = (w - w_hi.astype(F32)).astype(BF16)
    b = jnp.pad(b_router, (0, LANES - N_EXPERTS), constant_values=-1.0e30).reshape(1, LANES)
    return w_hi, w_lo, b


def kernel(x_prompt, x_sample, c, cache_ckv, cache_krope, state_fwd, state_bwd, c_ctx, w_mod, b_mod, g_norm1, g_norm2, mla_w_in, mla_g_q, mla_g_kv, mla_w_qb, mla_w_kvb, mla_w_o, gdn_w_in, gdn_conv, gdn_a_log, gdn_dt_bias, gdn_g_out, gdn_w_o, moe_w_router, moe_b_router, moe_w_gu, moe_b_gu, moe_w_down, moe_b_down, g_final):
    x = jnp.concatenate([x_prompt.reshape(N_PROMPT, D_MODEL), x_sample.reshape(N_LATENT, D_MODEL)], axis=0)
    cond8 = jnp.concatenate([c_ctx[None, :], c, jnp.zeros((SUBLANES - 1 - N_LATENT_SEQ, D_MODEL), F32)], axis=0)
    mod = _modulation(cond8, w_mod, b_mod)
    mod0 = mod[0].reshape(SUBLANES, 1, 6 * D_MODEL)
    mod1 = mod[1].reshape(SUBLANES, 1, 6 * D_MODEL)
    perm = _split_perm()

    w_in_p = jnp.pad(mla_w_in[0], ((0, 0), (0, LANES - MLA_ROPE))).astype(BF16)
    w_qb_p = jnp.pad(mla_w_qb[0].reshape(MLA_Q_LORA, N_HEADS, MLA_NOPE + MLA_ROPE),
                     ((0, 0), (0, 0), (0, HEAD_PAD - MLA_NOPE - MLA_ROPE))).reshape(MLA_Q_LORA, -1).astype(BF16)
    w_kvb4 = mla_w_kvb[0].reshape(MLA_KV_LORA, N_HEADS, 2, MLA_NOPE)
    w_kvb_p = jnp.concatenate([w_kvb4[:, :, 0, :].reshape(MLA_KV_LORA, -1),
                               w_kvb4[:, :, 1, :].reshape(MLA_KV_LORA, -1)], axis=1).astype(BF16)
    cos, sin = _rope_tables()
    q, kcat, v, ckv_n, kr = _mla_front(x, mod0, g_norm1[0:1], w_in_p, mla_g_q[0:1], mla_g_kv[0:1],
                                       w_qb_p, w_kvb_p, cos, sin)
    kcat_c, v_c = _cache_kv(cache_ckv[:, 0].reshape(N_LATENT_SEQ * PAST_LEN, MLA_KV_LORA),
                            jnp.pad(cache_krope[:, 0].reshape(N_LATENT_SEQ * PAST_LEN, MLA_ROPE),
                                    ((0, 0), (0, LANES - MLA_ROPE))), w_kvb_p)
    shp = lambda a, b, t: a.reshape(b, t, a.shape[-1])
    o_p = _attention(shp(q[:N_PROMPT], N_PROMPT_SEQ, PROMPT_LEN),
                     [(shp(kcat[:N_PROMPT], N_PROMPT_SEQ, PROMPT_LEN), shp(v[:N_PROMPT], N_PROMPT_SEQ, PROMPT_LEN))],
                     tq=PROMPT_LEN, heads=N_HEADS)
    o_s = _attention(shp(q[N_PROMPT:], N_LATENT_SEQ, LATENT_LEN),
                     [(shp(kcat_c, N_LATENT_SEQ, PAST_LEN), shp(v_c, N_LATENT_SEQ, PAST_LEN)),
                      (shp(kcat[N_PROMPT:], N_LATENT_SEQ, LATENT_LEN), shp(v[N_PROMPT:], N_LATENT_SEQ, LATENT_LEN))],
                     tq=256, heads=1)
    o = jnp.concatenate([o_p.reshape(N_PROMPT, -1), o_s.reshape(N_LATENT, -1)], axis=0)
    wr_hi, wr_lo, br = _router_operands(moe_w_router[0], moe_b_router[0])
    x1, hpk, eidx, gates, rank, counts = _post_mixer(o, mla_w_o[0].astype(BF16), x, mod0, g_norm2[0:1],
                                                     wr_hi, wr_lo, br)
    yg = _moe(hpk, eidx, rank, counts, moe_w_gu[0], moe_b_gu[0], moe_w_down[0], moe_b_down[0], perm)

    w_gdn = jnp.pad(gdn_w_in[0], ((0, 0), (0, GDN_IN_PAD - gdn_w_in.shape[-1]))).astype(BF16)
    x2, proj = _combine_gdn(x1, yg, gates, mod0, mod1, g_norm1[1:2], w_gdn)
    conv_w8 = jnp.pad(gdn_conv[0], ((0, SUBLANES - GDN_CONV), (0, 0)))
    lane_pad = (0, LANES - 2 * N_HEADS)
    alog = jnp.pad(gdn_a_log[0].reshape(-1), lane_pad).reshape(1, LANES)
    dtb = jnp.pad(gdn_dt_bias[0].reshape(-1), lane_pad).reshape(1, LANES)
    gout = gdn_g_out[0:1]
    y_p, sf, sb = _gdn(proj, conv_w8, alog, dtb, gout, 0, 1, PROMPT_LEN, PROMPT_LEN)
    y_s, _, _ = _gdn(proj, conv_w8, alog, dtb, gout, 1, N_LATENT_SEQ, LATENT_LEN, 1024,
                     state_fwd[:, 0], state_bwd[:, 0])
    yy = jnp.concatenate([y_p, y_s], axis=0)
    wr_hi, wr_lo, br = _router_operands(moe_w_router[1], moe_b_router[1])
    x3, hpk, eidx, gates, rank, counts = _post_mixer(yy, gdn_w_o[0].astype(BF16), x2, mod1, g_norm2[1:2],
                                                     wr_hi, wr_lo, br)
    yg = _moe(hpk, eidx, rank, counts, moe_w_gu[1], moe_b_gu[1], moe_w_down[1], moe_b_down[1], perm)
    y = _combine_final(x3, yg, gates, mod1, g_final.reshape(1, D_MODEL))

    y_prompt = y[:N_PROMPT].reshape(N_PROMPT_SEQ, PROMPT_LEN, D_MODEL)
    y_sample = y[N_PROMPT:].reshape(N_LATENT_SEQ, LATENT_LEN, D_MODEL)
    new_ckv = ckv_n[:N_PROMPT].reshape(N_PROMPT_SEQ, 1, PROMPT_LEN, MLA_KV_LORA)
    new_krope = kr[:N_PROMPT, :MLA_ROPE].reshape(N_PROMPT_SEQ, 1, PROMPT_LEN, MLA_ROPE)
    return (y_prompt, y_sample, new_ckv, new_krope, sf[:, None], sb[:, None])
```

```python
import functools
import math

import jax
import jax.numpy as jnp
from jax import lax
from jax.experimental import pallas as pl
from jax.experimental.pallas import tpu as pltpu
from jax.experimental.pallas import tpu_sc as plsc

F32 = jnp.float32
BF16 = jnp.bfloat16
U32 = jnp.uint32
I32 = jnp.int32

D_MODEL = 1024
N_PROMPT_SEQ = 16
PROMPT_LEN = 256
N_LATENT_SEQ = 2
LATENT_LEN = 4096
PAST_LEN = 256
GRID_W = 64
N_HEADS = 8
MLA_Q_LORA = 384
MLA_KV_LORA = 256
MLA_NOPE = 128
MLA_ROPE = 64
MLA_V = 128
ROPE_BASE = 10000.0
GDN_DK = 128
GDN_CONV = 5
GDN_CHUNK = 64
N_EXPERTS = 32
TOP_K = 4
D_EXPERT = 1024
SWIGLU_LIMIT = 7.0
SWIGLU_ALPHA = 1.702
EPS = 1e-6

LANES = 128
SUBLANES = 8
VMEM_LIMIT = 56 * 1024 * 1024

N_PROMPT = N_PROMPT_SEQ * PROMPT_LEN
N_LATENT = N_LATENT_SEQ * LATENT_LEN
N_TOK = N_PROMPT + N_LATENT
GROUP_ROWS = 4096
ROW_TILE = 256
HEAD_PAD = 256
MOE_BLOCK = 256
N_ASSIGN = N_TOK * TOP_K
N_BLOCKS = N_ASSIGN // MOE_BLOCK + N_EXPERTS
N_SORTED = N_BLOCKS * MOE_BLOCK
HALF = D_MODEL // 2
GDN_IN_PAD = 4 * D_MODEL + LANES
NEG_BIG = -3.0e38


def _cparams(sem):
    return pltpu.CompilerParams(dimension_semantics=sem, vmem_limit_bytes=VMEM_LIMIT)


def _dot(a, b):
    return jnp.dot(a, b, preferred_element_type=F32)


def _dot_nt(a, b):
    return lax.dot_general(a, b, (((1,), (1,)), ((), ())), preferred_element_type=F32)


def _rms(x, g):
    return x * lax.rsqrt(jnp.mean(x * x, axis=-1, keepdims=True) + EPS) * g


def _split3(x):
    a = x.astype(BF16)
    r = x - a.astype(F32)
    b = r.astype(BF16)
    c = (r - b.astype(F32)).astype(BF16)
    return a, b, c


def _dot3(a, b):
    a_hi = a.astype(BF16)
    a_lo = (a - a_hi.astype(F32)).astype(BF16)
    b_hi = b.astype(BF16)
    b_lo = (b - b_hi.astype(F32)).astype(BF16)
    return _dot(a_hi, b_hi) + _dot(a_hi, b_lo) + _dot(a_lo, b_hi)


def _pack_rows(x):
    hi = lax.bitcast_convert_type(x[:, :HALF].astype(BF16).astype(F32), U32)
    lo = lax.bitcast_convert_type(x[:, HALF:].astype(BF16).astype(F32), U32)
    return hi | (lo >> 16)


def _unpack_rows(u):
    hi = lax.bitcast_convert_type(u & jnp.uint32(0xFFFF0000), F32)
    lo = lax.bitcast_convert_type(u << 16, F32)
    return hi, lo


def _mod_kernel(c_ref, w_ref, b_ref, o_ref):
    c = c_ref[...]
    s = c * jax.nn.sigmoid(c)
    o_ref[0] = _dot(s.astype(BF16), w_ref[0].astype(BF16)) + b_ref[0]


def _modulation(cond8, w_mod, b_mod):
    depth, d, n = w_mod.shape
    tn = 768
    return pl.pallas_call(
        _mod_kernel,
        grid=(depth, n // tn),
        in_specs=[pl.BlockSpec((SUBLANES, d), lambda l, j: (0, 0)),
                  pl.BlockSpec((1, d, tn), lambda l, j: (l, 0, j)),
                  pl.BlockSpec((1, 1, tn), lambda l, j: (l, 0, j))],
        out_specs=pl.BlockSpec((1, SUBLANES, tn), lambda l, j: (l, 0, j)),
        out_shape=jax.ShapeDtypeStruct((depth, SUBLANES, n), F32),
        compiler_params=_cparams(("arbitrary", "arbitrary")),
        name="modulation",
    )(cond8, w_mod, b_mod.reshape(depth, 1, n))


def _rope_pad(xr, cos, sin, first_pair):
    up = pltpu.roll(xr, LANES - 16, 1)
    dn = pltpu.roll(xr, 16, 1)
    rot = jnp.where(first_pair, -up, dn)
    return xr * cos + rot * sin


def _keys_values(ckv_n, kr, w_kvb, kcat_ref, v_ref):
    kvp = _dot(ckv_n.astype(BF16), w_kvb)
    krb = kr.astype(BF16)
    for h in range(N_HEADS):
        kcat_ref[:, HEAD_PAD * h:HEAD_PAD * h + MLA_NOPE] = kvp[:, MLA_NOPE * h:MLA_NOPE * (h + 1)].astype(BF16)
        kcat_ref[:, HEAD_PAD * h + MLA_NOPE:HEAD_PAD * (h + 1)] = krb
    v_ref[...] = kvp[:, N_HEADS * MLA_NOPE:].astype(BF16)


def _mla_front_kernel(x_ref, mod_ref, g1_ref, win_ref, gq_ref, gkv_ref, wqb_ref, wkvb_ref, cos_ref, sin_ref,
                      q_ref, kcat_ref, v_ref, ckv_ref, kr_ref):
    m = mod_ref[0]
    sh, sc = m[:, 0:D_MODEL], m[:, D_MODEL:2 * D_MODEL]
    h = _rms(x_ref[...], g1_ref[...]) * (1.0 + sc) + sh
    a = _dot(h.astype(BF16), win_ref[...])
    qa = a[:, :MLA_Q_LORA]
    ckv = a[:, MLA_Q_LORA:MLA_Q_LORA + MLA_KV_LORA]
    kr = a[:, MLA_Q_LORA + MLA_KV_LORA:]
    q = _dot(_rms(qa, gq_ref[...]).astype(BF16), wqb_ref[...]) * ((MLA_NOPE + MLA_ROPE) ** -0.5)
    ckv_n = _rms(ckv, gkv_ref[...])
    cos, sin = cos_ref[...], sin_ref[...]
    lane = lax.broadcasted_iota(I32, (1, LANES), 1)
    first_pair = (lane % 32) < 16
    for hh in range(N_HEADS):
        lo = HEAD_PAD * hh
        q_ref[:, lo:lo + MLA_NOPE] = q[:, lo:lo + MLA_NOPE].astype(BF16)
        q_ref[:, lo + MLA_NOPE:lo + HEAD_PAD] = _rope_pad(q[:, lo + MLA_NOPE:lo + HEAD_PAD], cos, sin,
                                                         first_pair).astype(BF16)
    ckv_ref[...] = ckv_n
    kr_ref[...] = kr
    _keys_values(ckv_n, _rope_pad(kr, cos, sin, first_pair), wkvb_ref[...], kcat_ref, v_ref)


def _mla_front(x, mod0, g1, w_in_p, g_q, g_kv, w_qb_p, w_kvb_p, cos, sin):
    n = x.shape[0]
    tm = ROW_TILE
    tiles_per_group = GROUP_ROWS // tm
    row = lambda w: pl.BlockSpec((tm, w), lambda i: (i, 0))
    full = lambda a: pl.BlockSpec(a.shape, lambda i: (0,) * a.ndim)
    return pl.pallas_call(
        _mla_front_kernel,
        grid=(n // tm,),
        in_specs=[row(D_MODEL),
                  pl.BlockSpec((1, 1, 6 * D_MODEL), lambda i: (i // tiles_per_group, 0, 0)),
                  full(g1), full(w_in_p), full(g_q), full(g_kv), full(w_qb_p), full(w_kvb_p),
                  row(LANES), row(LANES)],
        out_specs=[row(N_HEADS * HEAD_PAD), row(N_HEADS * HEAD_PAD), row(N_HEADS * MLA_V),
                   row(MLA_KV_LORA), row(LANES)],
        out_shape=[jax.ShapeDtypeStruct((n, N_HEADS * HEAD_PAD), BF16),
                   jax.ShapeDtypeStruct((n, N_HEADS * HEAD_PAD), BF16),
                   jax.ShapeDtypeStruct((n, N_HEADS * MLA_V), BF16),
                   jax.ShapeDtypeStruct((n, MLA_KV_LORA), F32),
                   jax.ShapeDtypeStruct((n, LANES), F32)],
        compiler_params=_cparams(("arbitrary",)),
        name="mla_front",
    )(x, mod0, g1, w_in_p, g_q, g_kv, w_qb_p, w_kvb_p, cos, sin)


def _cache_kv_kernel(ckv_ref, kr_ref, wkvb_ref, kcat_ref, v_ref):
    _keys_values(ckv_ref[...], kr_ref[...], wkvb_ref[...], kcat_ref, v_ref)


def _cache_kv(ckv, kr_pad, w_kvb_p):
    n = ckv.shape[0]
    return pl.pallas_call(
        _cache_kv_kernel,
        out_shape=[jax.ShapeDtypeStruct((n, N_HEADS * HEAD_PAD), BF16),
                   jax.ShapeDtypeStruct((n, N_HEADS * MLA_V), BF16)],
        compiler_params=pltpu.CompilerParams(vmem_limit_bytes=VMEM_LIMIT),
        name="cache_kv",
    )(ckv, kr_pad, w_kvb_p)


def _attn_kernel(*refs, n_parts, heads):
    q_ref = refs[0]
    kv_refs = refs[1:1 + 2 * n_parts]
    o_ref = refs[-1]
    for j in range(heads):
        q = q_ref[:, HEAD_PAD * j:HEAD_PAD * (j + 1)]
        scores = [_dot_nt(q, kv_refs[2 * p][:, HEAD_PAD * j:HEAD_PAD * (j + 1)]) for p in range(n_parts)]
        m = scores[0].max(axis=-1, keepdims=True)
        for s in scores[1:]:
            m = jnp.maximum(m, s.max(axis=-1, keepdims=True))
        probs = [jnp.exp(s - m) for s in scores]
        denom = probs[0].sum(axis=-1, keepdims=True)
        for p in probs[1:]:
            denom = denom + p.sum(axis=-1, keepdims=True)
        acc = None
        for p in range(n_parts):
            t = _dot(probs[p].astype(BF16), kv_refs[2 * p + 1][:, MLA_V * j:MLA_V * (j + 1)])
            acc = t if acc is None else acc + t
        o_ref[:, MLA_V * j:MLA_V * (j + 1)] = (acc / denom).astype(BF16)


def _attention(q, kv_parts, n_seq, seq_len, row0, tq, heads, out_prev=None):
    n_parts = len(kv_parts)
    qb0 = row0 // tq
    q_per_seq = seq_len // tq
    in_specs = [pl.BlockSpec((tq, HEAD_PAD * heads), lambda bi, hi, qi: (qb0 + bi * q_per_seq + qi, hi))]
    args = [q]
    for k, v, rows, first in kv_parts:
        in_specs.append(pl.BlockSpec((rows, HEAD_PAD * heads), lambda bi, hi, qi, first=first: (first + bi, hi)))
        in_specs.append(pl.BlockSpec((rows, MLA_V * heads), lambda bi, hi, qi, first=first: (first + bi, hi)))
        args += [k, v]
    aliases = {}
    if out_prev is not None:
        in_specs.append(pl.BlockSpec(memory_space=pl.ANY))
        args.append(out_prev)
        aliases = {len(args) - 1: 0}
    return pl.pallas_call(
        functools.partial(_attn_kernel, n_parts=n_parts, heads=heads),
        grid=(n_seq, N_HEADS // heads, q_per_seq),
        in_specs=in_specs,
        out_specs=pl.BlockSpec((tq, MLA_V * heads), lambda bi, hi, qi: (qb0 + bi * q_per_seq + qi, hi)),
        out_shape=jax.ShapeDtypeStruct((q.shape[0], N_HEADS * MLA_V), BF16),
        input_output_aliases=aliases,
        compiler_params=_cparams(("arbitrary", "arbitrary", "arbitrary")),
        name=f"attention_{n_parts}",
    )(*args)


def _post_mixer_kernel(o_ref, wo_ref, x_ref, mod_ref, g2_ref, wr_hi_ref, wr_lo_ref, br_ref,
                       x1_ref, hpk_ref, eidx_ref, gate_ref, rank_ref, cnt_ref, carry_ref):
    i = pl.program_id(0)
    tm = x_ref.shape[0]

    @pl.when(i == 0)
    def _():
        carry_ref[...] = jnp.zeros_like(carry_ref)

    m = mod_ref[0]
    gt1 = m[:, 2 * D_MODEL:3 * D_MODEL]
    sh2, sc2 = m[:, 3 * D_MODEL:4 * D_MODEL], m[:, 4 * D_MODEL:5 * D_MODEL]
    x1 = x_ref[...] + gt1 * _dot(o_ref[...], wo_ref[...])
    x1_ref[...] = x1
    h2 = _rms(x1, g2_ref[...]) * (1.0 + sc2) + sh2
    hpk_ref[...] = _pack_rows(h2)

    h_hi = h2.astype(BF16)
    h_lo = (h2 - h_hi.astype(F32)).astype(BF16)
    w_hi = wr_hi_ref[...]
    logits = _dot(h_hi, w_hi) + _dot(h_lo, w_hi) + _dot(h_hi, wr_lo_ref[...]) + br_ref[...]

    lane = lax.broadcasted_iota(I32, (tm, LANES), 1)
    vals, idxs = [], []
    l = logits
    for _k in range(TOP_K):
        mx = l.max(axis=-1, keepdims=True)
        ix = jnp.where(l == mx, lane, LANES).min(axis=-1, keepdims=True)
        vals.append(mx)
        idxs.append(ix)
        l = jnp.where(lane == ix, NEG_BIG, l)
    exps = [jnp.exp(v - vals[0]) for v in vals]
    denom = exps[0] + exps[1] + exps[2] + exps[3]

    hot = jnp.zeros((tm, LANES), F32)
    for ix in idxs:
        hot = hot + (lane == ix).astype(F32)
    r_io = lax.broadcasted_iota(I32, (tm, tm), 0)
    c_io = lax.broadcasted_iota(I32, (tm, tm), 1)
    tri = (c_io < r_io).astype(BF16)
    prefix = _dot(tri, hot.astype(BF16)) + carry_ref[0:1, :]
    carry_ref[0:1, :] = carry_ref[0:1, :] + hot.sum(axis=0, keepdims=True)

    eidx = jnp.zeros((tm, LANES), I32)
    gate = jnp.zeros((tm, LANES), F32)
    rank = jnp.zeros((tm, LANES), I32)
    for k in range(TOP_K):
        rk = jnp.where(lane == idxs[k], prefix, 0.0).sum(axis=-1, keepdims=True).astype(I32)
        eidx = jnp.where(lane == k, idxs[k], eidx)
        gate = jnp.where(lane == k, exps[k] / denom, gate)
        rank = jnp.where(lane == k, rk, rank)
    eidx_ref[...] = eidx
    gate_ref[...] = gate
    rank_ref[...] = rank
    cnt_ref[...] = jnp.broadcast_to(carry_ref[0:1, :], cnt_ref.shape)


def _post_mixer(o, w_o, x, mod_l, g2, wr_hi, wr_lo, br):
    n = x.shape[0]
    tm = ROW_TILE
    tiles_per_group = GROUP_ROWS // tm
    row = lambda w: pl.BlockSpec((tm, w), lambda i: (i, 0))
    full = lambda a: pl.BlockSpec(a.shape, lambda i: (0,) * a.ndim)
    return pl.pallas_call(
        _post_mixer_kernel,
        grid=(n // tm,),
        in_specs=[row(D_MODEL), full(w_o), row(D_MODEL),
                  pl.BlockSpec((1, 1, 6 * D_MODEL), lambda i: (i // tiles_per_group, 0, 0)),
                  full(g2), full(wr_hi), full(wr_lo), full(br)],
        out_specs=[row(D_MODEL), row(HALF), row(LANES), row(LANES), row(LANES),
                   pl.BlockSpec((SUBLANES, LANES), lambda i: (0, 0))],
        out_shape=[jax.ShapeDtypeStruct((n, D_MODEL), F32),
                   jax.ShapeDtypeStruct((n, HALF), U32),
                   jax.ShapeDtypeStruct((n, LANES), I32),
                   jax.ShapeDtypeStruct((n, LANES), F32),
                   jax.ShapeDtypeStruct((n, LANES), I32),
                   jax.ShapeDtypeStruct((SUBLANES, LANES), F32)],
        scratch_shapes=[pltpu.VMEM((SUBLANES, LANES), F32)],
        compiler_params=_cparams(("arbitrary",)),
        name="post_mixer",
    )(o, w_o, x, mod_l, g2, wr_hi, wr_lo, br)


def _invert_kernel(pos_ref, rowtok_ref):
    def zero(r, c):
        rowtok_ref[r] = 0
        return c
    lax.fori_loop(0, N_SORTED, zero, 0)

    def put(a, c):
        rowtok_ref[pos_ref[a]] = a // TOP_K
        return c
    lax.fori_loop(0, N_ASSIGN, put, 0)


def _invert_positions(pos_flat):
    return pl.pallas_call(
        _invert_kernel,
        in_specs=[pl.BlockSpec(memory_space=pltpu.SMEM)],
        out_specs=pl.BlockSpec(memory_space=pltpu.SMEM),
        out_shape=jax.ShapeDtypeStruct((N_SORTED,), I32),
        name="invert_positions",
    )(pos_flat)


GATHER_ROWS = 256


def _gather_kernel(idx_ref, tab_ref, o_ref, sem):
    base = pl.program_id(0) * GATHER_ROWS

    def row_copy(r, src_row):
        return pltpu.make_async_copy(tab_ref.at[pl.ds(src_row, 1)], o_ref.at[pl.ds(r, 1)], sem)

    def issue(r, c):
        row_copy(r, idx_ref[base + r]).start()
        return c
    lax.fori_loop(0, GATHER_ROWS, issue, 0)

    def drain(r, c):
        row_copy(r, 0).wait()
        return c
    lax.fori_loop(0, GATHER_ROWS, drain, 0)


def _gather_rows(idx, table):
    n = idx.shape[0]
    return pl.pallas_call(
        _gather_kernel,
        grid_spec=pltpu.PrefetchScalarGridSpec(
            num_scalar_prefetch=1,
            grid=(n // GATHER_ROWS,),
            in_specs=[pl.BlockSpec(memory_space=pl.ANY)],
            out_specs=pl.BlockSpec((GATHER_ROWS, HALF), lambda i, idx_ref: (i, 0)),
            scratch_shapes=[pltpu.SemaphoreType.DMA(())]),
        out_shape=jax.ShapeDtypeStruct((n, HALF), U32),
        compiler_params=_cparams(("arbitrary",)),
        name="gather_rows",
    )(idx, table)


SC_WINDOW = 128


def _sc_gather_rows(idx, table):
    n = idx.shape[0]
    info = plsc.get_sparse_core_info()
    n_workers = info.num_cores * info.num_subcores
    per_worker = n // n_workers
    assert per_worker * n_workers == n and per_worker % SC_WINDOW == 0
    mesh = plsc.VectorSubcoreMesh(core_axis_name="core", subcore_axis_name="subcore")

    @functools.partial(pl.kernel, out_type=jax.ShapeDtypeStruct((n, HALF), I32), mesh=mesh,
                       scratch_types=[pltpu.VMEM((per_worker,), I32), pltpu.VMEM((SC_WINDOW, HALF), I32)],
                       name="sc_gather_rows")
    def gather(tab_hbm, idx_hbm, out_hbm, idx_v, rows_v):
        worker = lax.axis_index("subcore") * info.num_cores + lax.axis_index("core")
        base = worker * per_worker
        pltpu.sync_copy(idx_hbm.at[pl.ds(base, per_worker)], idx_v)

        @pl.loop(0, per_worker // SC_WINDOW)
        def _(w):
            off = w * SC_WINDOW
            pltpu.sync_copy(tab_hbm.at[idx_v.at[pl.ds(off, SC_WINDOW)]], rows_v)
            pltpu.sync_copy(rows_v, out_hbm.at[pl.ds(base + off, SC_WINDOW)])

    out = gather(lax.bitcast_convert_type(table, I32), idx)
    return lax.bitcast_convert_type(out, U32)


def _expert_kernel(be_ref, nu_ref, xs_ref, wgu_ref, bg_ref, bu_ref, wd_ref, bd_ref, perm_ref,
                   y_ref, wg_s, wu_s, wd_s):
    b = pl.program_id(0)
    e = be_ref[b]
    prev = be_ref[jnp.maximum(b - 1, 0)]

    @pl.when((b == 0) | (e != prev))
    def _():
        perm = perm_ref[...]
        for c in range(2 * D_EXPERT // HEAD_PAD):
            t = _dot(wgu_ref[0, 0, :, HEAD_PAD * c:HEAD_PAD * (c + 1)].astype(BF16), perm)
            wg_s[:, LANES * c:LANES * (c + 1)] = t[:, :LANES].astype(BF16)
            wu_s[:, LANES * c:LANES * (c + 1)] = t[:, LANES:].astype(BF16)
        wd_s[...] = wd_ref[0, 0].astype(BF16)

    @pl.when(b < nu_ref[0])
    def _():
        hi, lo = _unpack_rows(xs_ref[...])
        x = jnp.concatenate([hi.astype(BF16), lo.astype(BF16)], axis=1)
        gate = jnp.minimum(_dot(x, wg_s[...]) + bg_ref[0], SWIGLU_LIMIT)
        up = jnp.clip(_dot(x, wu_s[...]) + bu_ref[0], -SWIGLU_LIMIT, SWIGLU_LIMIT)
        hid = (up + 1.0) * (gate * jax.nn.sigmoid(SWIGLU_ALPHA * gate))
        y_ref[...] = _pack_rows(_dot(hid.astype(BF16), wd_s[...]) + bd_ref[0])

    @pl.when(b >= nu_ref[0])
    def _():
        y_ref[...] = jnp.zeros_like(y_ref)


def _experts(layer, block_e, n_used, xs, w_gu, b_gate, b_up, w_down, b_down, perm):
    return pl.pallas_call(
        _expert_kernel,
        grid_spec=pltpu.PrefetchScalarGridSpec(
            num_scalar_prefetch=2,
            grid=(N_BLOCKS,),
            in_specs=[pl.BlockSpec((MOE_BLOCK, HALF), lambda b, be, nu: (b, 0)),
                      pl.BlockSpec((1, 1, D_MODEL, 2 * D_EXPERT), lambda b, be, nu: (layer, be[b], 0, 0)),
                      pl.BlockSpec((1, 1, D_EXPERT), lambda b, be, nu: (be[b], 0, 0)),
                      pl.BlockSpec((1, 1, D_EXPERT), lambda b, be, nu: (be[b], 0, 0)),
                      pl.BlockSpec((1, 1, D_EXPERT, D_MODEL), lambda b, be, nu: (layer, be[b], 0, 0)),
                      pl.BlockSpec((1, 1, D_MODEL), lambda b, be, nu: (be[b], 0, 0)),
                      pl.BlockSpec((HEAD_PAD, HEAD_PAD), lambda b, be, nu: (0, 0))],
            out_specs=pl.BlockSpec((MOE_BLOCK, HALF), lambda b, be, nu: (b, 0)),
            scratch_shapes=[pltpu.VMEM((D_MODEL, D_EXPERT), BF16),
                            pltpu.VMEM((D_MODEL, D_EXPERT), BF16),
                            pltpu.VMEM((D_EXPERT, D_MODEL), BF16)]),
        out_shape=jax.ShapeDtypeStruct((N_SORTED, HALF), U32),
        compiler_params=_cparams(("arbitrary",)),
        name="experts",
    )(block_e, n_used, xs, w_gu, b_gate, b_up, w_down, b_down, perm)


def _moe(layer, hpk, eidx, rank, counts, w_gu, b_gu, w_down, b_down, perm):
    cnt = counts[0, :N_EXPERTS].astype(I32)
    padded = (cnt + MOE_BLOCK - 1) // MOE_BLOCK * MOE_BLOCK
    ends = jnp.cumsum(padded)
    starts = ends - padded
    pos = starts[eidx[:, :TOP_K]] + rank[:, :TOP_K]
    block_start = jnp.arange(N_BLOCKS, dtype=I32) * MOE_BLOCK
    block_e = jnp.minimum(jnp.sum(ends[None, :] <= block_start[:, None], axis=1), N_EXPERTS - 1).astype(I32)
    n_used = (ends[-1:] // MOE_BLOCK).astype(I32)
    row_tok = jnp.zeros((N_SORTED,), I32).at[pos.reshape(-1)].set(jnp.arange(N_ASSIGN, dtype=I32) // TOP_K)
    xs = _sc_gather_rows(row_tok, hpk)
    b_gate = b_gu[layer, :, 0::2].reshape(N_EXPERTS, 1, D_EXPERT)
    b_up = b_gu[layer, :, 1::2].reshape(N_EXPERTS, 1, D_EXPERT)
    ys = _experts(layer, block_e, n_used, xs, w_gu, b_gate, b_up, w_down,
                  b_down[layer].reshape(N_EXPERTS, 1, D_MODEL), perm)
    return _sc_gather_rows(pos.T.reshape(-1), ys).reshape(TOP_K, N_TOK, HALF)


def _moe_sum(yg_ref, gate_ref):
    g = gate_ref[...]
    acc_hi = acc_lo = None
    for k in range(TOP_K):
        hi, lo = _unpack_rows(yg_ref[k])
        gk = g[:, k:k + 1]
        acc_hi = gk * hi if acc_hi is None else acc_hi + gk * hi
        acc_lo = gk * lo if acc_lo is None else acc_lo + gk * lo
    return jnp.concatenate([acc_hi, acc_lo], axis=1)


def _combine_gdn_kernel(x1_ref, yg_ref, gate_ref, mod0_ref, mod1_ref, g1_ref, win_ref, x2_ref, proj_ref):
    gt2 = mod0_ref[0][:, 5 * D_MODEL:6 * D_MODEL]
    x2 = x1_ref[...] + gt2 * _moe_sum(yg_ref, gate_ref)
    x2_ref[...] = x2
    m1 = mod1_ref[0]
    h = _rms(x2, g1_ref[...]) * (1.0 + m1[:, D_MODEL:2 * D_MODEL]) + m1[:, 0:D_MODEL]
    proj_ref[...] = _dot(h.astype(BF16), win_ref[...])


def _combine_gdn(x1, yg, gates, mod0, mod1, g1, w_in):
    n = x1.shape[0]
    tm = ROW_TILE
    tiles_per_group = GROUP_ROWS // tm
    row = lambda w: pl.BlockSpec((tm, w), lambda i: (i, 0))
    full = lambda a: pl.BlockSpec(a.shape, lambda i: (0,) * a.ndim)
    modspec = pl.BlockSpec((1, 1, 6 * D_MODEL), lambda i: (i // tiles_per_group, 0, 0))
    return pl.pallas_call(
        _combine_gdn_kernel,
        grid=(n // tm,),
        in_specs=[row(D_MODEL), pl.BlockSpec((TOP_K, tm, HALF), lambda i: (0, i, 0)), row(LANES), modspec, modspec,
                  full(g1), full(w_in)],
        out_specs=[row(D_MODEL), row(GDN_IN_PAD)],
        out_shape=[jax.ShapeDtypeStruct((n, D_MODEL), F32),
                   jax.ShapeDtypeStruct((n, GDN_IN_PAD), F32)],
        compiler_params=_cparams(("arbitrary",)),
        name="combine_gdn_proj",
    )(x1, yg, gates, mod0, mod1, g1, w_in)


def _combine_final_kernel(x1_ref, yg_ref, gate_ref, mod_ref, gf_ref, y_ref):
    gt2 = mod_ref[0][:, 5 * D_MODEL:6 * D_MODEL]
    x2 = x1_ref[...] + gt2 * _moe_sum(yg_ref, gate_ref)
    y_ref[...] = _rms(x2, gf_ref[...])


def _combine_final(x1, yg, gates, mod_l, g_final):
    n = x1.shape[0]
    tm = ROW_TILE
    tiles_per_group = GROUP_ROWS // tm
    row = lambda w: pl.BlockSpec((tm, w), lambda i: (i, 0))
    return pl.pallas_call(
        _combine_final_kernel,
        grid=(n // tm,),
        in_specs=[row(D_MODEL), pl.BlockSpec((TOP_K, tm, HALF), lambda i: (0, i, 0)), row(LANES),
                  pl.BlockSpec((1, 1, 6 * D_MODEL), lambda i: (i // tiles_per_group, 0, 0)),
                  pl.BlockSpec(g_final.shape, lambda i: (0, 0))],
        out_specs=row(D_MODEL),
        out_shape=jax.ShapeDtypeStruct((n, D_MODEL), F32),
        compiler_params=_cparams(("arbitrary",)),
        name="combine_final",
    )(x1, yg, gates, mod_l, g_final)


GDN_TILE = 256
GDN_ROWS = 4096
CHUNKS_PER_TILE = GDN_TILE // GDN_CHUNK
CONV_PAD = 8


def _gdn_kernel(*refs, seq_len, sb_rows, has_init, has_prev):
    q_in, k_in, v_in, z_in, ab_in, cwq, cwk, cwv, alog, dtb, gout = refs[:11]
    s0f_ref, s0b_ref = refs[11:13] if has_init else (None, None)
    n_in = 11 + (2 if has_init else 0) + (1 if has_prev else 0)
    y_ref, sf_ref, sb_ref = refs[n_in:n_in + 3]
    (xpad, qn, kn, vv, gb, o_f, o_b, lhs_s, co_s, cs_s, d_s, st_f, st_b) = refs[n_in + 3:]

    head = pl.program_id(1)
    n_tiles = GDN_ROWS // GDN_TILE
    n_sb = GDN_ROWS // sb_rows
    ncs = sb_rows // GDN_CHUNK
    tiles_per_sb = sb_rows // GDN_TILE
    sb_per_seq = seq_len // sb_rows

    row_t = lax.broadcasted_iota(I32, (GDN_TILE, 1), 0)
    xpad[0:CONV_PAD, :] = jnp.zeros((CONV_PAD, LANES), F32)
    xpad[CONV_PAD + GDN_ROWS:, :] = jnp.zeros((CONV_PAD, LANES), F32)
    half = (GDN_CONV - 1) // 2
    for src, cw, dst, unit, scale in ((q_in, cwq, qn, True, GDN_DK ** -0.5), (k_in, cwk, kn, True, 1.0),
                                      (v_in, cwv, vv, False, 1.0)):
        xpad[CONV_PAD:CONV_PAD + GDN_ROWS, :] = src[...]
        w = cw[...]
        for t in range(n_tiles):
            r0 = t * GDN_TILE
            pos = (r0 + row_t) % seq_len
            acc = jnp.zeros((GDN_TILE, LANES), F32)
            for j in range(GDN_CONV):
                s = j - half
                xs = xpad[CONV_PAD + r0 + s:CONV_PAD + r0 + s + GDN_TILE, :]
                ok = (pos + s >= 0) & (pos + s < seq_len)
                acc = acc + jnp.where(ok, xs, 0.0) * w[j:j + 1, :]
            acc = acc * jax.nn.sigmoid(acc)
            if unit:
                acc = acc * lax.rsqrt(jnp.sum(acc * acc, axis=-1, keepdims=True) + EPS) * scale
            dst[r0:r0 + GDN_TILE, :] = acc
    lane1 = lax.broadcasted_iota(I32, (1, LANES), 1)
    ab = ab_in[...]
    sp = ab + dtb[...]
    softplus = jnp.maximum(sp, 0.0) + jnp.log(1.0 + jnp.exp(-jnp.abs(sp)))
    gb[...] = jnp.where(lane1 < 2 * N_HEADS, -jnp.exp(alog[...]) * softplus, jax.nn.sigmoid(ab))

    r_io = lax.broadcasted_iota(I32, (GDN_TILE, GDN_TILE), 0)
    c_io = lax.broadcasted_iota(I32, (GDN_TILE, GDN_TILE), 1)
    same = (r_io // GDN_CHUNK) == (c_io // GDN_CHUNK)
    eye = (r_io == c_io).astype(F32)
    lane_t = lax.broadcasted_iota(I32, (GDN_TILE, LANES), 1)
    colc = lax.broadcasted_iota(I32, (LANES, GDN_TILE), 1) // GDN_CHUNK

    def prep_tile(direction, r0, slot0):
        if direction == 0:
            incl = same & (c_io <= r_io)
            strict = same & (c_io < r_io)
        else:
            incl = same & (c_io >= r_io)
            strict = same & (c_io > r_io)
        q = qn[pl.ds(r0, GDN_TILE), :]
        k = kn[pl.ds(r0, GDN_TILE), :]
        v = vv[pl.ds(r0, GDN_TILE), :]
        gbt = gb[pl.ds(r0, GDN_TILE), :]
        col_g = N_HEADS * direction + head
        g_col = jnp.where(lane_t == col_g, gbt, 0.0).sum(axis=-1, keepdims=True)
        beta = jnp.where(lane_t == col_g + 2 * N_HEADS, gbt, 0.0).sum(axis=-1, keepdims=True)
        g_b = jnp.broadcast_to(g_col, (GDN_TILE, LANES))
        g3 = _split3(g_b)
        cs_m = incl.astype(BF16)
        all_m = same.astype(BF16)
        gc = _dot(cs_m, g3[0]) + _dot(cs_m, g3[1]) + _dot(cs_m, g3[2])
        gl = _dot(all_m, g3[0]) + _dot(all_m, g3[1]) + _dot(all_m, g3[2])
        gc_row = gc.T[0:1, :]
        diff = jnp.concatenate([gc, gc], axis=1) - gc_row
        decay = jnp.where(incl, jnp.exp(jnp.where(incl, diff, 0.0)), 0.0)
        e_gc = jnp.exp(gc)
        kb = k * beta
        kbf = k.astype(BF16)
        a = jnp.where(strict, _dot_nt(kb.astype(BF16), kbf) * decay, 0.0)
        tinv = eye - jnp.where((r_io // 2) == (c_io // 2), a, 0.0)
        size = 2
        while size < GDN_CHUNK:
            off = jnp.where(((r_io // (2 * size)) == (c_io // (2 * size))) & ((r_io // size) != (c_io // size)),
                            a, 0.0)
            tinv = tinv - _dot3(tinv, _dot3(off, tinv))
            size *= 2
        rhs = jnp.concatenate([v * beta, kb * e_gc], axis=1)
        uw = _dot3(tinv, rhs)
        intra = jnp.where(incl, _dot_nt(q.astype(BF16), kbf) * decay, 0.0)
        uwb = uw.astype(BF16)
        iuw = _dot(intra.astype(BF16), uwb)
        c_o = iuw[:, :LANES]
        q_eff = q * e_gc - iuw[:, LANES:]
        kd_t = (k * jnp.exp(gl - gc)).T
        dec = jnp.exp(gl)
        for mch in range(CHUNKS_PER_TILE):
            lo = mch * GDN_CHUNK
            gcs = _dot(jnp.where(colc == mch, kd_t, 0.0).astype(BF16), uwb)
            slot = slot0 + mch
            lhs_s[direction, slot, 0:GDN_CHUNK, :] = q_eff[lo:lo + GDN_CHUNK, :].astype(BF16)
            lhs_s[direction, slot, GDN_CHUNK:, :] = gcs[:, LANES:].astype(BF16)
            co_s[direction, slot] = c_o[lo:lo + GDN_CHUNK, :]
            cs_s[direction, slot] = gcs[:, :LANES]
            d_s[direction, slot] = dec[lo:lo + SUBLANES, :]

    def superblock(j, carry):
        sbs = (j, n_sb - 1 - j)
        for direction in (0, 1):
            base = sbs[direction] * sb_rows
            for t in range(tiles_per_sb):
                prep_tile(direction, pl.multiple_of(base + t * GDN_TILE, GDN_TILE), t * CHUNKS_PER_TILE)
        for direction, st, s0 in ((0, st_f, s0f_ref), (1, st_b, s0b_ref)):
            sbi = sbs[direction]
            first = (sbi % sb_per_seq == 0) if direction == 0 else ((sbi + 1) % sb_per_seq == 0)

            @pl.when(first)
            def _():
                st[...] = s0[0, 0] if has_init else jnp.zeros((GDN_DK, LANES), F32)

        def step(i, c):
            for direction, st, o_s in ((0, st_f, o_f), (1, st_b, o_b)):
                slot = i if direction == 0 else ncs - 1 - i
                s = st[...]
                r = _dot(lhs_s[direction, slot], s.astype(BF16))
                row0 = pl.multiple_of(sbs[direction] * sb_rows + slot * GDN_CHUNK, GDN_CHUNK)
                o_s[pl.ds(row0, GDN_CHUNK), :] = r[:GDN_CHUNK, :] + co_s[direction, slot]
                st[...] = d_s[direction, slot][0:1, :] * s - r[GDN_CHUNK:, :] + cs_s[direction, slot]
            return c
        lax.fori_loop(0, ncs, step, 0)

        for direction, st, out in ((0, st_f, sf_ref), (1, st_b, sb_ref)):
            sbi = sbs[direction]
            last = ((sbi + 1) % sb_per_seq == 0) if direction == 0 else (sbi % sb_per_seq == 0)

            @pl.when(last)
            def _():
                out[sbi // sb_per_seq, 0] = st[...]
        return carry
    lax.fori_loop(0, n_sb, superblock, 0)

    for t in range(n_tiles):
        r0 = t * GDN_TILE
        o = o_f[r0:r0 + GDN_TILE, :] + o_b[r0:r0 + GDN_TILE, :]
        z = z_in[r0:r0 + GDN_TILE, :]
        y_ref[r0:r0 + GDN_TILE, :] = (_rms(o, gout[...]) * (z * jax.nn.sigmoid(z))).astype(BF16)


def _gdn(proj, conv_w8, alog, dtb, gout, row_block0, n_blocks, seq_len, sb_rows, s0f=None, s0b=None, y_prev=None):
    has_init = s0f is not None
    has_prev = y_prev is not None
    seqs = GDN_ROWS // seq_len
    ncs = sb_rows // GDN_CHUNK
    col = lambda off: pl.BlockSpec((GDN_ROWS, LANES), lambda b, h: (b + row_block0, off + h))
    cwspec = lambda off: pl.BlockSpec((SUBLANES, LANES), lambda b, h: (0, off + h))
    vec = pl.BlockSpec((1, LANES), lambda b, h: (0, 0))
    in_specs = [col(0), col(N_HEADS), col(2 * N_HEADS), col(3 * N_HEADS),
                pl.BlockSpec((GDN_ROWS, LANES), lambda b, h: (b + row_block0, 4 * N_HEADS)),
                cwspec(0), cwspec(N_HEADS), cwspec(2 * N_HEADS), vec, vec, vec]
    args = [proj, proj, proj, proj, proj, conv_w8, conv_w8, conv_w8, alog, dtb, gout]
    if has_init:
        st_spec = pl.BlockSpec((1, 1, GDN_DK, LANES), lambda b, h: (b, h, 0, 0))
        in_specs += [st_spec, st_spec]
        args += [s0f, s0b]
    aliases = {}
    if has_prev:
        in_specs.append(pl.BlockSpec(memory_space=pl.ANY))
        args.append(y_prev)
        aliases = {len(args) - 1: 0}
    out_state = pl.BlockSpec((seqs, 1, GDN_DK, LANES), lambda b, h: (b, h, 0, 0))
    return pl.pallas_call(
        functools.partial(_gdn_kernel, seq_len=seq_len, sb_rows=sb_rows, has_init=has_init, has_prev=has_prev),
        grid=(n_blocks, N_HEADS),
        in_specs=in_specs,
        out_specs=[pl.BlockSpec((GDN_ROWS, LANES), lambda b, h: (b + row_block0, h)), out_state, out_state],
        input_output_aliases=aliases,
        out_shape=[jax.ShapeDtypeStruct((proj.shape[0], N_HEADS * LANES), BF16),
                   jax.ShapeDtypeStruct((n_blocks * seqs, N_HEADS, GDN_DK, LANES), F32),
                   jax.ShapeDtypeStruct((n_blocks * seqs, N_HEADS, GDN_DK, LANES), F32)],
        scratch_shapes=[pltpu.VMEM((GDN_ROWS + 2 * CONV_PAD, LANES), F32),
                        pltpu.VMEM((GDN_ROWS, LANES), F32), pltpu.VMEM((GDN_ROWS, LANES), F32),
                        pltpu.VMEM((GDN_ROWS, LANES), F32), pltpu.VMEM((GDN_ROWS, LANES), F32),
                        pltpu.VMEM((GDN_ROWS, LANES), F32), pltpu.VMEM((GDN_ROWS, LANES), F32),
                        pltpu.VMEM((2, ncs, GDN_CHUNK + GDN_DK, LANES), BF16),
                        pltpu.VMEM((2, ncs, GDN_CHUNK, LANES), F32),
                        pltpu.VMEM((2, ncs, GDN_DK, LANES), F32),
                        pltpu.VMEM((2, ncs, SUBLANES, LANES), F32),
                        pltpu.VMEM((GDN_DK, LANES), F32), pltpu.VMEM((GDN_DK, LANES), F32)],
        compiler_params=_cparams(("arbitrary", "arbitrary")),
        name=f"gdn_{seq_len}",
    )(*args)


def _rope_tables():
    rows = LATENT_LEN // GRID_W
    row = jnp.repeat(jnp.arange(rows, dtype=F32), GRID_W)
    colp = jnp.tile(jnp.arange(GRID_W, dtype=F32), rows)
    quarter = MLA_ROPE // 4
    inv = ROPE_BASE ** (-jnp.arange(quarter, dtype=F32) / quarter)
    ar = row[:, None] * inv
    ac = colp[:, None] * inv
    ang = jnp.concatenate([ar, ar, ac, ac], axis=-1)
    pad = ((0, 0), (0, LANES - MLA_ROPE))
    cos = jnp.pad(jnp.cos(ang), pad, constant_values=1.0)
    sin = jnp.pad(jnp.sin(ang), pad)
    cos = jnp.concatenate([jnp.ones((N_PROMPT, LANES), F32)] + [cos] * N_LATENT_SEQ, axis=0)
    sin = jnp.concatenate([jnp.zeros((N_PROMPT, LANES), F32)] + [sin] * N_LATENT_SEQ, axis=0)
    return cos, sin


def _split_perm():
    src = jnp.arange(HEAD_PAD)
    dst = jnp.where(src % 2 == 0, src // 2, LANES + src // 2)
    return (dst[:, None] == jnp.arange(HEAD_PAD)[None, :]).astype(BF16)


def _router_operands(w_router, b_router):
    w = jnp.pad(w_router, ((0, 0), (0, LANES - N_EXPERTS)))
    w_hi = w.astype(BF16)
    w_lo = (w - w_hi.astype(F32)).astype(BF16)
    b = jnp.pad(b_router, (0, LANES - N_EXPERTS), constant_values=-1.0e30).reshape(1, LANES)
    return w_hi, w_lo, b


def kernel(x_prompt, x_sample, c, cache_ckv, cache_krope, state_fwd, state_bwd, c_ctx, w_mod, b_mod, g_norm1, g_norm2, mla_w_in, mla_g_q, mla_g_kv, mla_w_qb, mla_w_kvb, mla_w_o, gdn_w_in, gdn_conv, gdn_a_log, gdn_dt_bias, gdn_g_out, gdn_w_o, moe_w_router, moe_b_router, moe_w_gu, moe_b_gu, moe_w_down, moe_b_down, g_final):
    x = jnp.concatenate([x_prompt.reshape(N_PROMPT, D_MODEL), x_sample.reshape(N_LATENT, D_MODEL)], axis=0)
    cond8 = jnp.concatenate([c_ctx[None, :], c, jnp.zeros((SUBLANES - 1 - N_LATENT_SEQ, D_MODEL), F32)], axis=0)
    mod = _modulation(cond8, w_mod, b_mod)
    mod0 = mod[0].reshape(SUBLANES, 1, 6 * D_MODEL)
    mod1 = mod[1].reshape(SUBLANES, 1, 6 * D_MODEL)
    perm = _split_perm()

    w_in_p = jnp.pad(mla_w_in[0], ((0, 0), (0, LANES - MLA_ROPE))).astype(BF16)
    w_qb_p = jnp.pad(mla_w_qb[0].reshape(MLA_Q_LORA, N_HEADS, MLA_NOPE + MLA_ROPE),
                     ((0, 0), (0, 0), (0, HEAD_PAD - MLA_NOPE - MLA_ROPE))).reshape(MLA_Q_LORA, -1).astype(BF16)
    w_kvb4 = mla_w_kvb[0].reshape(MLA_KV_LORA, N_HEADS, 2, MLA_NOPE)
    w_kvb_p = jnp.concatenate([w_kvb4[:, :, 0, :].reshape(MLA_KV_LORA, -1),
                               w_kvb4[:, :, 1, :].reshape(MLA_KV_LORA, -1)], axis=1).astype(BF16)
    cos, sin = _rope_tables()
    q, kcat, v, ckv_n, kr = _mla_front(x, mod0, g_norm1[0:1], w_in_p, mla_g_q[0:1], mla_g_kv[0:1],
                                       w_qb_p, w_kvb_p, cos, sin)
    kcat_c, v_c = _cache_kv(cache_ckv[:, 0].reshape(N_LATENT_SEQ * PAST_LEN, MLA_KV_LORA),
                            jnp.pad(cache_krope[:, 0].reshape(N_LATENT_SEQ * PAST_LEN, MLA_ROPE),
                                    ((0, 0), (0, LANES - MLA_ROPE))), w_kvb_p)
    o = _attention(q, [(kcat, v, PROMPT_LEN, 0)], N_PROMPT_SEQ, PROMPT_LEN, 0, PROMPT_LEN, N_HEADS)
    o = _attention(q, [(kcat_c, v_c, PAST_LEN, 0), (kcat, v, LATENT_LEN, N_PROMPT // LATENT_LEN)],
                   N_LATENT_SEQ, LATENT_LEN, N_PROMPT, 256, 1, out_prev=o)
    wr_hi, wr_lo, br = _router_operands(moe_w_router[0], moe_b_router[0])
    x1, hpk, eidx, gates, rank, counts = _post_mixer(o, mla_w_o[0].astype(BF16), x, mod0, g_norm2[0:1],
                                                     wr_hi, wr_lo, br)
    yg = _moe(0, hpk, eidx, rank, counts, moe_w_gu, moe_b_gu, moe_w_down, moe_b_down, perm)

    w_gdn = jnp.pad(gdn_w_in[0], ((0, 0), (0, GDN_IN_PAD - gdn_w_in.shape[-1]))).astype(BF16)
    x2, proj = _combine_gdn(x1, yg, gates, mod0, mod1, g_norm1[1:2], w_gdn)
    conv_w8 = jnp.pad(gdn_conv[0], ((0, SUBLANES - GDN_CONV), (0, 0)))
    lane_pad = (0, LANES - 2 * N_HEADS)
    alog = jnp.pad(gdn_a_log[0].reshape(-1), lane_pad).reshape(1, LANES)
    dtb = jnp.pad(gdn_dt_bias[0].reshape(-1), lane_pad).reshape(1, LANES)
    gout = gdn_g_out[0:1]
    yy, sf, sb = _gdn(proj, conv_w8, alog, dtb, gout, 0, 1, PROMPT_LEN, PROMPT_LEN)
    yy, _, _ = _gdn(proj, conv_w8, alog, dtb, gout, 1, N_LATENT_SEQ, LATENT_LEN, 1024,
                    state_fwd[:, 0], state_bwd[:, 0], y_prev=yy)
    wr_hi, wr_lo, br = _router_operands(moe_w_router[1], moe_b_router[1])
    x3, hpk, eidx, gates, rank, counts = _post_mixer(yy, gdn_w_o[0].astype(BF16), x2, mod1, g_norm2[1:2],
                                                     wr_hi, wr_lo, br)
    yg = _moe(1, hpk, eidx, rank, counts, moe_w_gu, moe_b_gu, moe_w_down, moe_b_down, perm)
    y = _combine_final(x3, yg, gates, mod1, g_final.reshape(1, D_MODEL))

    y_prompt = y[:N_PROMPT].reshape(N_PROMPT_SEQ, PROMPT_LEN, D_MODEL)
    y_sample = y[N_PROMPT:].reshape(N_LATENT_SEQ, LATENT_LEN, D_MODEL)
    new_ckv = ckv_n[:N_PROMPT].reshape(N_PROMPT_SEQ, 1, PROMPT_LEN, MLA_KV_LORA)
    new_krope = kr[:N_PROMPT, :MLA_ROPE].reshape(N_PROMPT_SEQ, 1, PROMPT_LEN, MLA_ROPE)
    return (y_prompt, y_sample, new_ckv, new_krope, sf[:, None], sb[:, None])
```
